```python
import math
import jax, jax.numpy as jnp
from jax import lax
import numpy as np

D_MODEL = 1024
BATCH = 16
SEQ = 2048
DEPTH = 1

HEAD_DIM = 64
MOBA_HEADS = 8
SB_HEADS = 8
MOBA_WIDTH = MOBA_HEADS * HEAD_DIM
SB_WIDTH = SB_HEADS * HEAD_DIM
MOBA_BLOCK = 256
MOBA_TOPK = 3
MOBA_Q_CHUNK = 16
SB_Q_BLOCK = 128
MEM_TOKENS = 256
X_HEADS = 4
X_HEAD_DIM = 128
X_WIDTH = X_HEADS * X_HEAD_DIM
D_FF = 4 * D_MODEL
IN_COLS = 3 * MOBA_WIDTH + 3 * SB_WIDTH + 2 * D_MODEL
RMS_EPS = 1e-6
NEG = -1e30

kernel_name = "hybrid_moba_stickbreak_gated_block"


def rmsnorm(x, g):
    x32 = x.astype(jnp.float32)
    y = x32 * lax.rsqrt(jnp.mean(x32 * x32, axis=-1, keepdims=True) + RMS_EPS)
    return (y * g.astype(jnp.float32)).astype(x.dtype)


def split_heads(t, n):
    B, S, _ = t.shape
    return t.reshape(B, S, n, -1).transpose(0, 2, 1, 3)


def merge_heads(t):
    B, H, S, dh = t.shape
    return t.transpose(0, 2, 1, 3).reshape(B, S, H * dh)


def alibi_slopes(n_heads):
    return jnp.asarray([2.0 ** (-8.0 * (h + 1) / n_heads) for h in range(n_heads)], dtype=jnp.float32)


def moba_attention(q, k, v):
    B, H, S, dh = q.shape
    scale = dh ** -0.5
    nb = -(-S // MOBA_BLOCK)
    pad = nb * MOBA_BLOCK - S
    kp = jnp.pad(k, ((0, 0), (0, 0), (0, pad), (0, 0)))
    vp = jnp.pad(v, ((0, 0), (0, 0), (0, pad), (0, 0)))
    kb = kp.reshape(B, H, nb, MOBA_BLOCK, dh)
    vb = vp.reshape(B, H, nb, MOBA_BLOCK, dh)
    k_mean = jnp.mean(kb.astype(jnp.float32), axis=3)
    slopes = alibi_slopes(H)
    topk = min(MOBA_TOPK, nb)
    nc = S // MOBA_Q_CHUNK
    q_chunks = q.reshape(B, H, nc, MOBA_Q_CHUNK, dh).transpose(2, 0, 1, 3, 4)
    bi = jnp.arange(B)[:, None, None, None]
    hi = jnp.arange(H)[None, :, None, None]
    blk_ar = jnp.arange(MOBA_BLOCK)

    def chunk(args):
        c, qx = args
        start = c * MOBA_Q_CHUNK
        t = (start + jnp.arange(MOBA_Q_CHUNK)).astype(jnp.int32)
        blk = start // MOBA_BLOCK
        qx32 = qx.astype(jnp.float32)
        gate = jnp.einsum('bhqd,bhnd->bhqn', qx32, k_mean)
        gate = jnp.where(jnp.arange(nb) < blk, gate, NEG)
        _, idx = lax.top_k(gate, topk)
        valid = idx < blk
        k_sel = kb[bi, hi, idx]
        v_sel = vb[bi, hi, idx]
        s_sel = jnp.einsum('bhqd,bhqnkd->bhqnk', qx, k_sel).astype(jnp.float32) * scale
        k_pos = idx[..., None] * MOBA_BLOCK + blk_ar
        dist = (t[None, None, :, None, None] - k_pos).astype(jnp.float32)
        s_sel = s_sel - slopes[None, :, None, None, None] * dist
        s_sel = jnp.where(valid[..., None], s_sel, NEG).reshape(B, H, MOBA_Q_CHUNK, topk * MOBA_BLOCK)
        k_own = lax.dynamic_slice_in_dim(kp, blk * MOBA_BLOCK, MOBA_BLOCK, axis=2)
        v_own = lax.dynamic_slice_in_dim(vp, blk * MOBA_BLOCK, MOBA_BLOCK, axis=2)
        own_pos = blk * MOBA_BLOCK + blk_ar
        d_own = (t[:, None] - own_pos[None, :]).astype(jnp.float32)
        s_own = jnp.einsum('bhqd,bhkd->bhqk', qx, k_own).astype(jnp.float32) * scale
        s_own = s_own - slopes[None, :, None, None] * d_own
        s_own = jnp.where(own_pos[None, :] <= t[:, None], s_own, NEG)
        p = jax.nn.softmax(jnp.concatenate([s_sel, s_own], axis=-1), axis=-1)
        p_sel = p[..., :topk * MOBA_BLOCK].reshape(B, H, MOBA_Q_CHUNK, topk, MOBA_BLOCK).astype(v.dtype)
        p_own = p[..., topk * MOBA_BLOCK:].astype(v.dtype)
        return (jnp.einsum('bhqnk,bhqnkd->bhqd', p_sel, v_sel)
                + jnp.einsum('bhqk,bhkd->bhqd', p_own, v_own))

    out = lax.map(chunk, (jnp.arange(nc, dtype=jnp.int32), q_chunks))
    return out.transpose(1, 2, 0, 3, 4).reshape(B, H, S, dh)


def stick_breaking_attention(q, k, v):
    B, H, S, dh = q.shape
    scale = dh ** -0.5
    outs = []
    for start in range(0, S, SB_Q_BLOCK):
        end = start + SB_Q_BLOCK
        qx = q[:, :, start:end]
        kx = k[:, :, :end]
        vx = v[:, :, :end]
        z = jnp.einsum('bhqd,bhkd->bhqk', qx, kx).astype(jnp.float32) * scale
        t = start + jnp.arange(SB_Q_BLOCK)
        s = jnp.arange(end)
        causal = s[None, :] < t[:, None]
        sp = jnp.where(causal, jax.nn.softplus(z), 0.0)
        log_rest = lax.cumsum(sp, axis=3, reverse=True) - sp
        a = jnp.where(causal, jnp.exp(jax.nn.log_sigmoid(z) - log_rest), 0.0)
        outs.append(jnp.einsum('bhqk,bhkd->bhqd', a.astype(v.dtype), vx))
    return jnp.concatenate(outs, axis=2)


def cross_attention(h, mem_n, w_q, w_kv, w_o):
    q = split_heads(h @ w_q, X_HEADS)
    kv = mem_n @ w_kv
    k = split_heads(kv[..., :X_WIDTH], X_HEADS)
    v = split_heads(kv[..., X_WIDTH:], X_HEADS)
    s = jnp.einsum('bhqd,bhkd->bhqk', q, k).astype(jnp.float32) * (X_HEAD_DIM ** -0.5)
    p = jax.nn.softmax(s, axis=-1).astype(v.dtype)
    return merge_heads(jnp.einsum('bhqk,bhkd->bhqd', p, v)) @ w_o


def setup_inputs(seed: int = 0) -> dict:
    key = jax.random.key(seed)
    ks = jax.random.split(key, 20)

    def w(k, shape, fan_in):
        return jax.random.normal(k, shape, jnp.float32) * (fan_in ** -0.5)

    def gain(k, shape):
        return 1.0 + 0.02 * jax.random.normal(k, shape, jnp.float32)

    L = DEPTH
    return {
        "x": jax.random.normal(ks[0], (BATCH, SEQ, D_MODEL), jnp.float32),
        "mem": jax.random.normal(ks[1], (BATCH, MEM_TOKENS, D_MODEL), jnp.float32),
        "g_mix": gain(ks[2], (L, D_MODEL)),
        "w_in": w(ks[3], (L, D_MODEL, IN_COLS), D_MODEL),
        "w_br_moba": w(ks[4], (L, MOBA_WIDTH, D_MODEL), MOBA_WIDTH),
        "w_br_sb": w(ks[5], (L, SB_WIDTH, D_MODEL), SB_WIDTH),
        "w_out": w(ks[6], (L, D_MODEL, D_MODEL), D_MODEL),
        "g_cross": gain(ks[7], (L, D_MODEL)),
        "g_mem": gain(ks[8], (L, D_MODEL)),
        "w_xq": w(ks[9], (L, D_MODEL, X_WIDTH), D_MODEL),
        "w_xkv": w(ks[10], (L, D_MODEL, 2 * X_WIDTH), D_MODEL),
        "w_xo": w(ks[11], (L, X_WIDTH, D_MODEL), X_WIDTH),
        "g_mlp": gain(ks[12], (L, D_MODEL)),
        "w_up": w(ks[13], (L, D_MODEL, D_FF), D_MODEL),
        "w_down": w(ks[14], (L, D_FF, D_MODEL), D_FF),
        "g_final": gain(ks[15], (D_MODEL,)),
    }


def reference(x, mem, g_mix, w_in, w_br_moba, w_br_sb, w_out, g_cross, g_mem,
              w_xq, w_xkv, w_xo, g_mlp, w_up, w_down, g_final):
    o1 = 3 * MOBA_WIDTH
    o2 = o1 + 3 * SB_WIDTH
    o3 = o2 + D_MODEL
    for l in range(DEPTH):
        h = rmsnorm(x, g_mix[l])
        proj = h @ w_in[l]
        qa = split_heads(proj[..., 0:MOBA_WIDTH], MOBA_HEADS)
        ka = split_heads(proj[..., MOBA_WIDTH:2 * MOBA_WIDTH], MOBA_HEADS)
        va = split_heads(proj[..., 2 * MOBA_WIDTH:o1], MOBA_HEADS)
        qb = split_heads(proj[..., o1:o1 + SB_WIDTH], SB_HEADS)
        kbh = split_heads(proj[..., o1 + SB_WIDTH:o1 + 2 * SB_WIDTH], SB_HEADS)
        vbh = split_heads(proj[..., o1 + 2 * SB_WIDTH:o2], SB_HEADS)
        gate_a = jax.nn.sigmoid(proj[..., o2:o3])
        gate_b = jax.nn.sigmoid(proj[..., o3:])
        y_a = merge_heads(moba_attention(qa, ka, va)) @ w_br_moba[l]
        y_b = merge_heads(stick_breaking_attention(qb, kbh, vbh)) @ w_br_sb[l]
        x = x + (gate_a * y_a + gate_b * y_b) @ w_out[l]
        x = x + cross_attention(rmsnorm(x, g_cross[l]), rmsnorm(mem, g_mem[l]),
                                w_xq[l], w_xkv[l], w_xo[l])
        u = rmsnorm(x, g_mlp[l]) @ w_up[l]
        x = x + jnp.square(jax.nn.relu(u)) @ w_down[l]
    return rmsnorm(x, g_final)
```

```python
import functools

import jax
import jax.numpy as jnp
from jax import lax
from jax.experimental import pallas as pl
from jax.experimental.pallas import tpu as pltpu

F32 = jnp.float32
BF16 = jnp.bfloat16

HEAD_DIM = 64
MOBA_HEADS = 8
SB_HEADS = 8
MIX_WIDTH = MOBA_HEADS * HEAD_DIM
MOBA_BLOCK = 256
MOBA_TOPK = 3
X_HEADS = 4
X_HEAD_DIM = 128
X_WIDTH = X_HEADS * X_HEAD_DIM
RMS_EPS = 1e-6
NEG = -1e30

LANES = 128
SUBLANES = 8
HEADS_PER_TILE = LANES // HEAD_DIM
ATT_BLOCK = 256
ROW_TILE = 512
COL_CHUNK = 512
VMEM_LIMIT = 56 * 1024 * 1024


def _params(n_axes):
    return pltpu.CompilerParams(
        dimension_semantics=("arbitrary",) * n_axes, vmem_limit_bytes=VMEM_LIMIT)


def _resident(shape):
    return pl.BlockSpec(shape, lambda *_: (0,) * len(shape), pipeline_mode=pl.Buffered(1))


def _rms_scale(x, g):
    y = x * lax.rsqrt(jnp.mean(x * x, axis=-1, keepdims=True) + RMS_EPS)
    return y * g


def _dot(a, b):
    return jnp.dot(a, b, preferred_element_type=F32)


def _dot_nt(a, b):
    return lax.dot_general(a, b, (((1,), (1,)), ((), ())), preferred_element_type=F32)


def _in_proj_kernel(x_ref, g_ref, w_ref, o_ref, *, q_chunks, gate_start):
    h = _rms_scale(x_ref[...], g_ref[...]).astype(BF16)
    n_chunks = w_ref.shape[1] // COL_CHUNK
    for c in range(n_chunks):
        cols = slice(c * COL_CHUNK, (c + 1) * COL_CHUNK)
        acc = _dot(h, w_ref[:, cols])
        if c in q_chunks:
            acc = acc * (HEAD_DIM ** -0.5)
        if c >= gate_start:
            acc = 1.0 / (1.0 + jnp.exp(-acc))
        o_ref[:, cols] = acc.astype(o_ref.dtype)


def _in_proj(x2d, g, w):
    n, d = x2d.shape
    cols = w.shape[1]
    q_chunks = (0, 3 * MIX_WIDTH // COL_CHUNK)
    gate_start = 6 * MIX_WIDTH // COL_CHUNK
    return pl.pallas_call(
        functools.partial(_in_proj_kernel, q_chunks=q_chunks, gate_start=gate_start),
        out_shape=jax.ShapeDtypeStruct((n, cols), BF16),
        grid=(n // ROW_TILE,),
        in_specs=[
            pl.BlockSpec((ROW_TILE, d), lambda i: (i, 0)),
            _resident((1, d)),
            _resident((d, cols)),
        ],
        out_specs=pl.BlockSpec((ROW_TILE, cols), lambda i: (i, 0)),
        compiler_params=_params(1),
        name="in_proj",
    )(x2d, g, w)


def _moba_kernel(q_ref, k_ref, v_ref, kaux_ref, qaux_ref, o_ref,
                 kaug_ref, vaug_ref, kmt_ref):
    seq = q_ref.shape[1]
    nb = seq // ATT_BLOCK
    lane = lax.broadcasted_iota(jnp.int32, (1, LANES), 1)
    row_t = lax.broadcasted_iota(jnp.int32, (ATT_BLOCK, ATT_BLOCK), 0)
    col_t = lax.broadcasted_iota(jnp.int32, (ATT_BLOCK, ATT_BLOCK), 1)
    causal = col_t <= row_t
    lane_q = lax.broadcasted_iota(jnp.int32, (ATT_BLOCK, LANES), 1)

    k_all = k_ref[0]
    v_all = v_ref[0]
    means = [jnp.mean(k_all[j * ATT_BLOCK:(j + 1) * ATT_BLOCK].astype(F32), axis=0, keepdims=True)
             for j in range(nb)]
    nb_rows = -(-nb // SUBLANES) * SUBLANES
    if nb_rows > nb:
        means.append(jnp.zeros((nb_rows - nb, LANES), F32))
    means = jnp.concatenate(means, axis=0)
    for hh in range(HEADS_PER_TILE):
        own = (lane >= hh * HEAD_DIM) & (lane < (hh + 1) * HEAD_DIM)
        spare0 = (1 - hh) * HEAD_DIM
        kaug_ref[hh] = jnp.where(own, k_all, kaux_ref[hh])
        vaug_ref[hh] = jnp.where(own, v_all, jnp.ones_like(v_all))
        parts = []
        if spare0:
            parts.append(jnp.zeros((spare0, LANES), F32))
        parts.append(jnp.where(own, means, 0.0))
        rest = LANES - spare0 - nb_rows
        if rest:
            parts.append(jnp.zeros((rest, LANES), F32))
        kmt_ref[hh] = jnp.concatenate(parts, axis=0).astype(BF16)

    for i in range(nb):
        q_i = q_ref[0, i * ATT_BLOCK:(i + 1) * ATT_BLOCK, :]
        outs = []
        for hh in range(HEADS_PER_TILE):
            own_q = (lane_q >= hh * HEAD_DIM) & (lane_q < (hh + 1) * HEAD_DIM)
            spare0 = (1 - hh) * HEAD_DIM
            jidx = lane_q - spare0
            past = (jidx >= 0) & (jidx < i)
            if i > MOBA_TOPK:
                gate = _dot_nt(q_i, kmt_ref[hh])
                beaten = jnp.zeros((ATT_BLOCK, LANES), F32)
                for jp in range(i):
                    g_jp = jnp.sum(jnp.where(jidx == jp, gate, 0.0), axis=1, keepdims=True)
                    wins = (g_jp > gate) | ((g_jp == gate) & (jp < jidx))
                    beaten = beaten + jnp.where(wins, 1.0, 0.0)
                chosen = past & (beaten < MOBA_TOPK)
            else:
                chosen = past
            attend = chosen | (jidx == i)
            is_block_lane = (jidx >= 0) & (jidx < nb)
            spare_vals = jnp.where(is_block_lane, jnp.where(attend, 0.0, NEG),
                                   qaux_ref[0, hh:hh + 1, :])
            q_aug = jnp.where(own_q, q_i, spare_vals.astype(BF16))

            tiles = []
            for j in range(i + 1):
                s = _dot_nt(q_aug, kaug_ref[hh, j * ATT_BLOCK:(j + 1) * ATT_BLOCK, :])
                if j == i:
                    s = jnp.where(causal, s, NEG)
                tiles.append(s)
            m_el = tiles[0]
            for s in tiles[1:]:
                m_el = jnp.maximum(m_el, s)
            m = jnp.max(m_el, axis=1, keepdims=True)
            acc = jnp.zeros((ATT_BLOCK, LANES), F32)
            for j in range(i + 1):
                p = jnp.exp(tiles[j] - m).astype(BF16)
                acc = acc + _dot(p, vaug_ref[hh, j * ATT_BLOCK:(j + 1) * ATT_BLOCK, :])
            denom = pltpu.roll(acc, HEAD_DIM, axis=1)
            outs.append(acc / denom)
        o_pair = jnp.where(lane_q < HEAD_DIM, outs[0], outs[1])
        o_ref[0, i * ATT_BLOCK:(i + 1) * ATT_BLOCK, :] = o_pair.astype(o_ref.dtype)


def _moba_aux(seq):
    nb = seq // MOBA_BLOCK
    pos = jnp.arange(seq)
    lane = jnp.arange(LANES)
    kaux, qaux = [], []
    for hh in range(HEADS_PER_TILE):
        spare0 = (1 - hh) * HEAD_DIM
        j = lane[None, :] - spare0
        ind = (j == (pos // MOBA_BLOCK)[:, None]).astype(F32)
        off = jnp.where(j == nb, (pos % MOBA_BLOCK)[:, None], 0).astype(F32)
        start = jnp.where(j == nb + 1, ((pos // MOBA_BLOCK) * MOBA_BLOCK)[:, None], 0).astype(F32)
        kaux.append(ind + off + start)
        qaux.append(((j == nb) | (j == nb + 1))[0].astype(F32))
    kaux = jnp.stack(kaux).astype(BF16)
    qaux = jnp.stack(qaux)
    slopes = jnp.asarray([2.0 ** (-8.0 * (h + 1) / MOBA_HEADS) for h in range(MOBA_HEADS)], F32)
    slopes = slopes.reshape(MOBA_HEADS // HEADS_PER_TILE, HEADS_PER_TILE, 1)
    return kaux, qaux[None] * slopes


def _moba(proj3, q_col, k_col, v_col):
    b, seq, _ = proj3.shape
    nb = seq // MOBA_BLOCK
    assert seq % MOBA_BLOCK == 0 and nb + 2 <= HEAD_DIM
    pairs = MOBA_HEADS // HEADS_PER_TILE
    kaux, qaux = _moba_aux(seq)
    blk = lambda c0: pl.BlockSpec((1, seq, LANES), lambda bi, p: (bi, 0, c0 + p))
    return pl.pallas_call(
        _moba_kernel,
        out_shape=jax.ShapeDtypeStruct((b, seq, MIX_WIDTH), BF16),
        grid=(b, pairs),
        in_specs=[
            blk(q_col // LANES), blk(k_col // LANES), blk(v_col // LANES),
            _resident((HEADS_PER_TILE, seq, LANES)),
            pl.BlockSpec((1, HEADS_PER_TILE, LANES), lambda bi, p: (p, 0, 0)),
        ],
        out_specs=pl.BlockSpec((1, seq, LANES), lambda bi, p: (bi, 0, p)),
        scratch_shapes=[
            pltpu.VMEM((HEADS_PER_TILE, seq, LANES), BF16),
            pltpu.VMEM((HEADS_PER_TILE, seq, LANES), BF16),
            pltpu.VMEM((HEADS_PER_TILE, LANES, LANES), BF16),
        ],
        compiler_params=_params(2),
        name="moba",
    )(proj3, proj3, proj3, kaux, qaux)


def _sb_kernel(q_ref, k_ref, v_ref, tri_ref, o_ref):
    seq = q_ref.shape[1]
    nb = seq // ATT_BLOCK
    row_t = lax.broadcasted_iota(jnp.int32, (ATT_BLOCK, ATT_BLOCK), 0)
    col_t = lax.broadcasted_iota(jnp.int32, (ATT_BLOCK, ATT_BLOCK), 1)
    strict = col_t < row_t
    lane_q = lax.broadcasted_iota(jnp.int32, (ATT_BLOCK, LANES), 1)
    tri = tri_ref[...]

    for i in range(nb):
        q_i = q_ref[0, i * ATT_BLOCK:(i + 1) * ATT_BLOCK, :]
        outs = []
        for hh in range(HEADS_PER_TILE):
            own_q = (lane_q >= hh * HEAD_DIM) & (lane_q < (hh + 1) * HEAD_DIM)
            q_own = jnp.where(own_q, q_i, jnp.zeros_like(q_i))
            rest = jnp.zeros((ATT_BLOCK, 1), F32)
            acc = jnp.zeros((ATT_BLOCK, LANES), F32)
            for j in range(i, -1, -1):
                rows = slice(j * ATT_BLOCK, (j + 1) * ATT_BLOCK)
                z = _dot_nt(q_own, k_ref[0, rows, :])
                sp = jnp.maximum(z, 0.0) + jnp.log1p(jnp.exp(-jnp.abs(z)))
                if j == i:
                    sp = jnp.where(strict, sp, 0.0)
                sp_hi = sp.astype(BF16)
                sp_lo = (sp - sp_hi.astype(F32)).astype(BF16)
                suffix = _dot(sp_hi, tri) + _dot(sp_lo, tri)
                a = jnp.exp(z - suffix - rest)
                if j == i:
                    a = jnp.where(strict, a, 0.0)
                acc = acc + _dot(a.astype(BF16), v_ref[0, rows, :])
                rest = rest + suffix[:, 0:1]
            outs.append(acc)
        o_pair = jnp.where(lane_q < HEAD_DIM, outs[0], outs[1])
        o_ref[0, i * ATT_BLOCK:(i + 1) * ATT_BLOCK, :] = o_pair.astype(o_ref.dtype)


def _stick_breaking(proj3, q_col, k_col, v_col):
    b, seq, _ = proj3.shape
    assert seq % ATT_BLOCK == 0
    pairs = SB_HEADS // HEADS_PER_TILE
    idx = jnp.arange(ATT_BLOCK)
    tri = (idx[:, None] >= idx[None, :]).astype(BF16)
    blk = lambda c0: pl.BlockSpec((1, seq, LANES), lambda bi, p: (bi, 0, c0 + p))
    return pl.pallas_call(
        _sb_kernel,
        out_shape=jax.ShapeDtypeStruct((b, seq, MIX_WIDTH), BF16),
        grid=(b, pairs),
        in_specs=[
            blk(q_col // LANES), blk(k_col // LANES), blk(v_col // LANES),
            _resident((ATT_BLOCK, ATT_BLOCK)),
        ],
        out_specs=pl.BlockSpec((1, seq, LANES), lambda bi, p: (bi, 0, p)),
        compiler_params=_params(2),
        name="stick_breaking",
    )(proj3, proj3, proj3, tri)


def _mix_out_kernel(a_ref, b_ref, ga_ref, gb_ref, x_ref, wa_ref, wb_ref, wo_ref, o_ref):
    ya = _dot(a_ref[...], wa_ref[...])
    yb = _dot(b_ref[...], wb_ref[...])
    merged = ga_ref[...].astype(F32) * ya + gb_ref[...].astype(F32) * yb
    o_ref[...] = x_ref[...] + _dot(merged.astype(BF16), wo_ref[...])


def _mix_out(moba_o, sb_o, proj, gate_col, x2d, wa, wb, wo):
    n, d = x2d.shape
    rows = lambda w, c=0: pl.BlockSpec((ROW_TILE, w), lambda i: (i, c))
    return pl.pallas_call(
        _mix_out_kernel,
        out_shape=jax.ShapeDtypeStruct((n, d), F32),
        grid=(n // ROW_TILE,),
        in_specs=[
            rows(MIX_WIDTH), rows(MIX_WIDTH),
            rows(d, gate_col // d), rows(d, gate_col // d + 1),
            rows(d),
            _resident(wa.shape), _resident(wb.shape), _resident(wo.shape),
        ],
        out_specs=rows(d),
        compiler_params=_params(1),
        name="mix_out",
    )(moba_o, sb_o, proj, proj, x2d, wa, wb, wo)


def _mem_kv_kernel(mem_ref, g_ref, w_ref, o_ref):
    h = _rms_scale(mem_ref[0], g_ref[...]).astype(BF16)
    o_ref[0] = _dot(h, w_ref[...]).astype(o_ref.dtype)


def _mem_kv(mem, g, w):
    b, m, d = mem.shape
    return pl.pallas_call(
        _mem_kv_kernel,
        out_shape=jax.ShapeDtypeStruct((b, m, w.shape[1]), BF16),
        grid=(b,),
        in_specs=[pl.BlockSpec((1, m, d), lambda i: (i, 0, 0)), _resident((1, d)), _resident(w.shape)],
        out_specs=pl.BlockSpec((1, m, w.shape[1]), lambda i: (i, 0, 0)),
        compiler_params=_params(1),
        name="mem_kv",
    )(mem, g, w)


def _cross_kernel(x_ref, kv_ref, g_ref, wq_ref, wo_ref, o_ref):
    x = x_ref[0]
    h = _rms_scale(x, g_ref[...]).astype(BF16)
    q = (_dot(h, wq_ref[...]) * (X_HEAD_DIM ** -0.5)).astype(BF16)
    heads = []
    for hd in range(X_HEADS):
        cols = slice(hd * X_HEAD_DIM, (hd + 1) * X_HEAD_DIM)
        k_h = kv_ref[0, :, cols]
        v_h = kv_ref[0, :, X_WIDTH + hd * X_HEAD_DIM:X_WIDTH + (hd + 1) * X_HEAD_DIM]
        s = _dot_nt(q[:, cols], k_h)
        p = jnp.exp(s - jnp.max(s, axis=1, keepdims=True))
        l = jnp.sum(p, axis=1, keepdims=True)
        heads.append((_dot(p.astype(BF16), v_h) / l).astype(BF16))
    attn = jnp.concatenate(heads, axis=1)
    o_ref[0] = x + _dot(attn, wo_ref[...])


def _cross(x3, kv, g, wq, wo):
    b, seq, d = x3.shape
    m = kv.shape[1]
    return pl.pallas_call(
        _cross_kernel,
        out_shape=jax.ShapeDtypeStruct((b, seq, d), F32),
        grid=(b, seq // ROW_TILE),
        in_specs=[
            pl.BlockSpec((1, ROW_TILE, d), lambda bi, i: (bi, i, 0)),
            pl.BlockSpec((1, m, kv.shape[2]), lambda bi, i: (bi, 0, 0)),
            _resident((1, d)), _resident(wq.shape), _resident(wo.shape),
        ],
        out_specs=pl.BlockSpec((1, ROW_TILE, d), lambda bi, i: (bi, i, 0)),
        compiler_params=_params(2),
        name="cross_attn",
    )(x3, kv, g, wq, wo)


def _mlp_kernel(x_ref, g_ref, wu_ref, wd_ref, gf_ref, o_ref, *, final_norm):
    x = x_ref[...]
    h = _rms_scale(x, g_ref[...]).astype(BF16)
    acc = x
    for c in range(wu_ref.shape[1] // COL_CHUNK):
        cols = slice(c * COL_CHUNK, (c + 1) * COL_CHUNK)
        u = jnp.maximum(_dot(h, wu_ref[:, cols]), 0.0)
        acc = acc + _dot((u * u).astype(BF16), wd_ref[cols, :])
    o_ref[...] = _rms_scale(acc, gf_ref[...]) if final_norm else acc


def _mlp(x2d, g, wu, wd, gf, final_norm):
    n, d = x2d.shape
    return pl.pallas_call(
        functools.partial(_mlp_kernel, final_norm=final_norm),
        out_shape=jax.ShapeDtypeStruct((n, d), F32),
        grid=(n // ROW_TILE,),
        in_specs=[
            pl.BlockSpec((ROW_TILE, d), lambda i: (i, 0)),
            _resident((1, d)), _resident(wu.shape), _resident(wd.shape), _resident((1, d)),
        ],
        out_specs=pl.BlockSpec((ROW_TILE, d), lambda i: (i, 0)),
        compiler_params=_params(1),
        name="mlp_final",
    )(x2d, g, wu, wd, gf)


def kernel(x, mem, g_mix, w_in, w_br_moba, w_br_sb, w_out, g_cross, g_mem,
           w_xq, w_xkv, w_xo, g_mlp, w_up, w_down, g_final):
    b, seq, d = x.shape
    n = b * seq
    assert n % ROW_TILE == 0 and seq % ROW_TILE == 0
    row = lambda v: v.reshape(1, -1).astype(F32)
    depth = g_mix.shape[0]
    x2d = x.reshape(n, d)
    for l in range(depth):
        proj = _in_proj(x2d, row(g_mix[l]), w_in[l].astype(BF16))
        proj3 = proj.reshape(b, seq, -1)
        moba_o = _moba(proj3, 0, MIX_WIDTH, 2 * MIX_WIDTH)
        sb_o = _stick_breaking(proj3, 3 * MIX_WIDTH, 4 * MIX_WIDTH, 5 * MIX_WIDTH)
        x2d = _mix_out(moba_o.reshape(n, -1), sb_o.reshape(n, -1), proj, 6 * MIX_WIDTH, x2d,
                       w_br_moba[l].astype(BF16), w_br_sb[l].astype(BF16), w_out[l].astype(BF16))
        kv = _mem_kv(mem, row(g_mem[l]), w_xkv[l].astype(BF16))
        x3 = _cross(x2d.reshape(b, seq, d), kv, row(g_cross[l]),
                    w_xq[l].astype(BF16), w_xo[l].astype(BF16))
        x2d = _mlp(x3.reshape(n, d), row(g_mlp[l]), w_up[l].astype(BF16), w_down[l].astype(BF16),
                   row(g_final), final_norm=(l == depth - 1))
    return x2d.reshape(b, seq, d)
```

```python
import functools

import jax
import jax.numpy as jnp
from jax import lax
from jax.experimental import pallas as pl
from jax.experimental.pallas import tpu as pltpu

F32 = jnp.float32
BF16 = jnp.bfloat16

HEAD_DIM = 64
MOBA_HEADS = 8
SB_HEADS = 8
MIX_WIDTH = MOBA_HEADS * HEAD_DIM
MOBA_BLOCK = 256
MOBA_TOPK = 3
X_HEADS = 4
X_HEAD_DIM = 128
X_WIDTH = X_HEADS * X_HEAD_DIM
RMS_EPS = 1e-6
NEG = -1e30

LANES = 128
SUBLANES = 8
HEADS_PER_TILE = LANES // HEAD_DIM
ATT_BLOCK = 256
ROW_TILE = 512
COL_CHUNK = 512
VMEM_LIMIT = 56 * 1024 * 1024


def _params(n_axes):
    return pltpu.CompilerParams(
        dimension_semantics=("arbitrary",) * n_axes, vmem_limit_bytes=VMEM_LIMIT)


def _resident(shape):
    return pl.BlockSpec(shape, lambda *_: (0,) * len(shape), pipeline_mode=pl.Buffered(1))


def _rms_scale(x, g):
    y = x * lax.rsqrt(jnp.mean(x * x, axis=-1, keepdims=True) + RMS_EPS)
    return y * g


def _dot(a, b):
    return jnp.dot(a, b, preferred_element_type=F32)


def _dot_nt(a, b):
    return lax.dot_general(a, b, (((1,), (1,)), ((), ())), preferred_element_type=F32)


def _in_proj_kernel(x_ref, g_ref, w_ref, o_ref, *, q_chunks, gate_start):
    h = _rms_scale(x_ref[...], g_ref[...]).astype(BF16)
    n_chunks = w_ref.shape[1] // COL_CHUNK
    for c in range(n_chunks):
        cols = slice(c * COL_CHUNK, (c + 1) * COL_CHUNK)
        acc = _dot(h, w_ref[:, cols])
        if c in q_chunks:
            acc = acc * (HEAD_DIM ** -0.5)
        if c >= gate_start:
            acc = 1.0 / (1.0 + jnp.exp(-acc))
        o_ref[:, cols] = acc.astype(o_ref.dtype)


def _in_proj(x2d, g, w):
    n, d = x2d.shape
    cols = w.shape[1]
    q_chunks = (0, 3 * MIX_WIDTH // COL_CHUNK)
    gate_start = 6 * MIX_WIDTH // COL_CHUNK
    return pl.pallas_call(
        functools.partial(_in_proj_kernel, q_chunks=q_chunks, gate_start=gate_start),
        out_shape=jax.ShapeDtypeStruct((n, cols), BF16),
        grid=(n // ROW_TILE,),
        in_specs=[
            pl.BlockSpec((ROW_TILE, d), lambda i: (i, 0)),
            _resident((1, d)),
            _resident((d, cols)),
        ],
        out_specs=pl.BlockSpec((ROW_TILE, cols), lambda i: (i, 0)),
        compiler_params=_params(1),
        name="in_proj",
    )(x2d, g, w)


def _moba_kernel(q_ref, k_ref, v_ref, kaux_ref, qaux_ref, o_ref,
                 kaug_ref, vaug_ref, kmt_ref):
    seq = q_ref.shape[1]
    nb = seq // ATT_BLOCK
    lane = lax.broadcasted_iota(jnp.int32, (1, LANES), 1)
    row_t = lax.broadcasted_iota(jnp.int32, (ATT_BLOCK, ATT_BLOCK), 0)
    col_t = lax.broadcasted_iota(jnp.int32, (ATT_BLOCK, ATT_BLOCK), 1)
    causal = col_t <= row_t
    lane_q = lax.broadcasted_iota(jnp.int32, (ATT_BLOCK, LANES), 1)

    k_all = k_ref[0]
    v_all = v_ref[0]
    means = [jnp.mean(k_all[j * ATT_BLOCK:(j + 1) * ATT_BLOCK].astype(F32), axis=0, keepdims=True)
             for j in range(nb)]
    nb_rows = -(-nb // SUBLANES) * SUBLANES
    if nb_rows > nb:
        means.append(jnp.zeros((nb_rows - nb, LANES), F32))
    means = jnp.concatenate(means, axis=0)
    for hh in range(HEADS_PER_TILE):
        own = (lane >= hh * HEAD_DIM) & (lane < (hh + 1) * HEAD_DIM)
        spare0 = (1 - hh) * HEAD_DIM
        kaug_ref[hh] = jnp.where(own, k_all, kaux_ref[hh])
        vaug_ref[hh] = jnp.where(own, v_all, jnp.ones_like(v_all))
        parts = []
        if spare0:
            parts.append(jnp.zeros((spare0, LANES), F32))
        parts.append(jnp.where(own, means, 0.0))
        rest = LANES - spare0 - nb_rows
        if rest:
            parts.append(jnp.zeros((rest, LANES), F32))
        kmt_ref[hh] = jnp.concatenate(parts, axis=0).astype(BF16)

    for i in range(nb):
        q_i = q_ref[0, i * ATT_BLOCK:(i + 1) * ATT_BLOCK, :]
        outs = []
        for hh in range(HEADS_PER_TILE):
            own_q = (lane_q >= hh * HEAD_DIM) & (lane_q < (hh + 1) * HEAD_DIM)
            spare0 = (1 - hh) * HEAD_DIM
            jidx = lane_q - spare0
            past = (jidx >= 0) & (jidx < i)
            if i > MOBA_TOPK:
                gate = _dot_nt(q_i, kmt_ref[hh])
                beaten = jnp.zeros((ATT_BLOCK, LANES), F32)
                for jp in range(i):
                    g_jp = jnp.sum(jnp.where(jidx == jp, gate, 0.0), axis=1, keepdims=True)
                    wins = (g_jp > gate) | ((g_jp == gate) & (jp < jidx))
                    beaten = beaten + jnp.where(wins, 1.0, 0.0)
                chosen = past & (beaten < MOBA_TOPK)
            else:
                chosen = past
            attend = chosen | (jidx == i)
            is_block_lane = (jidx >= 0) & (jidx < nb)
            spare_vals = jnp.where(is_block_lane, jnp.where(attend, 0.0, NEG),
                                   qaux_ref[0, hh:hh + 1, :])
            q_aug = jnp.where(own_q, q_i, spare_vals.astype(BF16))

            tiles = []
            for j in range(i + 1):
                s = _dot_nt(q_aug, kaug_ref[hh, j * ATT_BLOCK:(j + 1) * ATT_BLOCK, :])
                if j == i:
                    s = jnp.where(causal, s, NEG)
                tiles.append(s)
            m_el = tiles[0]
            for s in tiles[1:]:
                m_el = jnp.maximum(m_el, s)
            m = jnp.max(m_el, axis=1, keepdims=True)
            acc = jnp.zeros((ATT_BLOCK, LANES), F32)
            for j in range(i + 1):
                p = jnp.exp(tiles[j] - m).astype(BF16)
                acc = acc + _dot(p, vaug_ref[hh, j * ATT_BLOCK:(j + 1) * ATT_BLOCK, :])
            denom = pltpu.roll(acc, HEAD_DIM, axis=1)
            outs.append(acc / denom)
        o_pair = jnp.where(lane_q < HEAD_DIM, outs[0], outs[1])
        o_ref[0, i * ATT_BLOCK:(i + 1) * ATT_BLOCK, :] = o_pair.astype(o_ref.dtype)


def _moba_aux(seq):
    nb = seq // MOBA_BLOCK
    pos = jnp.arange(seq)
    lane = jnp.arange(LANES)
    kaux, qaux = [], []
    for hh in range(HEADS_PER_TILE):
        spare0 = (1 - hh) * HEAD_DIM
        j = lane[None, :] - spare0
        ind = (j == (pos // MOBA_BLOCK)[:, None]).astype(F32)
        off = jnp.where(j == nb, (pos % MOBA_BLOCK)[:, None], 0).astype(F32)
        start = jnp.where(j == nb + 1, ((pos // MOBA_BLOCK) * MOBA_BLOCK)[:, None], 0).astype(F32)
        kaux.append(ind + off + start)
        qaux.append(((j == nb) | (j == nb + 1))[0].astype(F32))
    kaux = jnp.stack(kaux).astype(BF16)
    qaux = jnp.stack(qaux)
    slopes = jnp.asarray([2.0 ** (-8.0 * (h + 1) / MOBA_HEADS) for h in range(MOBA_HEADS)], F32)
    slopes = slopes.reshape(MOBA_HEADS // HEADS_PER_TILE, HEADS_PER_TILE, 1)
    return kaux, qaux[None] * slopes


def _moba(proj3, q_col, k_col, v_col):
    b, seq, _ = proj3.shape
    nb = seq // MOBA_BLOCK
    assert seq % MOBA_BLOCK == 0 and nb + 2 <= HEAD_DIM
    pairs = MOBA_HEADS // HEADS_PER_TILE
    kaux, qaux = _moba_aux(seq)
    blk = lambda c0: pl.BlockSpec((1, seq, LANES), lambda bi, p: (bi, 0, c0 + p))
    return pl.pallas_call(
        _moba_kernel,
        out_shape=jax.ShapeDtypeStruct((b, seq, MIX_WIDTH), BF16),
        grid=(b, pairs),
        in_specs=[
            blk(q_col // LANES), blk(k_col // LANES), blk(v_col // LANES),
            _resident((HEADS_PER_TILE, seq, LANES)),
            pl.BlockSpec((1, HEADS_PER_TILE, LANES), lambda bi, p: (p, 0, 0)),
        ],
        out_specs=pl.BlockSpec((1, seq, LANES), lambda bi, p: (bi, 0, p)),
        scratch_shapes=[
            pltpu.VMEM((HEADS_PER_TILE, seq, LANES), BF16),
            pltpu.VMEM((HEADS_PER_TILE, seq, LANES), BF16),
            pltpu.VMEM((HEADS_PER_TILE, LANES, LANES), BF16),
        ],
        compiler_params=_params(2),
        name="moba",
    )(proj3, proj3, proj3, kaux, qaux)


SB_DEAD = 120.0
SB_SKEW = 2


def _sb_kernel(q_ref, k_ref, v_ref, tri_ref, o_ref, acc_ref, rest_ref):
    seq = q_ref.shape[1]
    nb = seq // ATT_BLOCK
    row_t = lax.broadcasted_iota(jnp.int32, (ATT_BLOCK, ATT_BLOCK), 0)
    col_t = lax.broadcasted_iota(jnp.int32, (ATT_BLOCK, ATT_BLOCK), 1)
    strict = col_t < row_t
    lane_q = lax.broadcasted_iota(jnp.int32, (ATT_BLOCK, LANES), 1)
    tri = tri_ref[...]
    key_rows = lambda j: slice(j * ATT_BLOCK, (j + 1) * ATT_BLOCK)

    def q_own(i, hh):
        q_i = q_ref[0, i * ATT_BLOCK:(i + 1) * ATT_BLOCK, :]
        own_q = (lane_q >= hh * HEAD_DIM) & (lane_q < (hh + 1) * HEAD_DIM)
        return jnp.where(own_q, q_i, jnp.zeros_like(q_i))

    def run_chains(chains, rests):
        n = len(chains)
        z, sp_split, suffix, a = {}, {}, {}, {}
        outs, new_rests = [None] * n, [None] * n

        def scores(c):
            i, hh, j = chains[c]
            z[c] = _dot_nt(q_own(i, hh), k_ref[0, key_rows(j), :])

        def softplus(c):
            i, hh, j = chains[c]
            neg_abs = lax.bitcast_convert_type(
                lax.bitcast_convert_type(z[c], jnp.uint32) | jnp.uint32(0x80000000), F32)
            sp = jnp.maximum(z[c], 0.0) + jnp.log(1.0 + jnp.exp(neg_abs))
            if j == i:
                sp = jnp.where(strict, sp, 0.0)
            hi = sp.astype(BF16)
            lo = (sp - hi.astype(F32)).astype(BF16)
            sp_split[c] = jnp.concatenate([hi, lo], axis=1)

        def suffix_sum(c):
            suffix[c] = _dot(sp_split[c], tri)

        def weights(c):
            i, hh, j = chains[c]
            w = jnp.exp(z[c] - suffix[c] - rests[c])
            if j == i:
                w = jnp.where(strict, w, 0.0)
            a[c] = w.astype(BF16)
            new_rests[c] = rests[c] + suffix[c][:, 0:1]

        def values(c):
            i, hh, j = chains[c]
            outs[c] = _dot(a[c], v_ref[0, key_rows(j), :])

        stages = (scores, softplus, suffix_sum, weights, values)
        for t in range(n + (len(stages) - 1) * SB_SKEW):
            for k, stage in enumerate(stages):
                c = t - k * SB_SKEW
                if 0 <= c < n:
                    stage(c)
        return outs, new_rests

    slot = lambda i, hh: i * HEADS_PER_TILE + hh
    units = lambda d: [(i, hh, i - d) for i in range(d, nb) for hh in range(HEADS_PER_TILE)]

    diag = units(0)
    acc, rest = run_chains(diag, [jnp.zeros((ATT_BLOCK, 1), F32)] * len(diag))
    acc = {(i, hh): o for (i, hh, _), o in zip(diag, acc)}
    rest = {(i, hh): r for (i, hh, _), r in zip(diag, rest)}
    if nb > 1:
        sub = units(1)
        outs, rests = run_chains(sub, [rest[(i, hh)] for (i, hh, _) in sub])
        for (i, hh, _), o, r in zip(sub, outs, rests):
            acc[(i, hh)] = acc[(i, hh)] + o
            rest[(i, hh)] = r
    for (i, hh), o in acc.items():
        acc_ref[slot(i, hh)] = o
        rest_ref[slot(i, hh)] = jnp.broadcast_to(rest[(i, hh)], (ATT_BLOCK, LANES))

    for d in range(2, nb):
        far = units(d)
        low = rest_ref[slot(d, 0)]
        for s in range(slot(d, 0) + 1, nb * HEADS_PER_TILE):
            low = jnp.minimum(low, rest_ref[s])

        @pl.when(jnp.min(low) < SB_DEAD)
        def _():
            outs, rests = run_chains(far, [rest_ref[slot(i, hh)][:, 0:1] for (i, hh, _) in far])
            for (i, hh, _), o, r in zip(far, outs, rests):
                acc_ref[slot(i, hh)] = acc_ref[slot(i, hh)] + o
                rest_ref[slot(i, hh)] = jnp.broadcast_to(r, (ATT_BLOCK, LANES))

    for i in range(nb):
        o_pair = jnp.where(lane_q < HEAD_DIM, acc_ref[slot(i, 0)], acc_ref[slot(i, 1)])
        o_ref[0, i * ATT_BLOCK:(i + 1) * ATT_BLOCK, :] = o_pair.astype(o_ref.dtype)


def _stick_breaking(proj3, q_col, k_col, v_col):
    b, seq, _ = proj3.shape
    assert seq % ATT_BLOCK == 0
    nb = seq // ATT_BLOCK
    pairs = SB_HEADS // HEADS_PER_TILE
    idx = jnp.arange(ATT_BLOCK)
    tri = (idx[:, None] >= idx[None, :]).astype(BF16)
    tri = jnp.concatenate([tri, tri], axis=0)
    blk = lambda c0: pl.BlockSpec((1, seq, LANES), lambda bi, p: (bi, 0, c0 + p))
    state = pltpu.VMEM((nb * HEADS_PER_TILE, ATT_BLOCK, LANES), F32)
    return pl.pallas_call(
        _sb_kernel,
        out_shape=jax.ShapeDtypeStruct((b, seq, MIX_WIDTH), BF16),
        grid=(b, pairs),
        in_specs=[
            blk(q_col // LANES), blk(k_col // LANES), blk(v_col // LANES),
            _resident(tri.shape),
        ],
        out_specs=pl.BlockSpec((1, seq, LANES), lambda bi, p: (bi, 0, p)),
        scratch_shapes=[state, state],
        compiler_params=_params(2),
        name="stick_breaking",
    )(proj3, proj3, proj3, tri)


def _mix_out_kernel(a_ref, b_ref, ga_ref, gb_ref, x_ref, wa_ref, wb_ref, wo_ref, o_ref):
    ya = _dot(a_ref[...], wa_ref[...])
    yb = _dot(b_ref[...], wb_ref[...])
    merged = ga_ref[...].astype(F32) * ya + gb_ref[...].astype(F32) * yb
    o_ref[...] = x_ref[...] + _dot(merged.astype(BF16), wo_ref[...])


def _mix_out(moba_o, sb_o, proj, gate_col, x2d, wa, wb, wo):
    n, d = x2d.shape
    rows = lambda w, c=0: pl.BlockSpec((ROW_TILE, w), lambda i: (i, c))
    return pl.pallas_call(
        _mix_out_kernel,
        out_shape=jax.ShapeDtypeStruct((n, d), F32),
        grid=(n // ROW_TILE,),
        in_specs=[
            rows(MIX_WIDTH), rows(MIX_WIDTH),
            rows(d, gate_col // d), rows(d, gate_col // d + 1),
            rows(d),
            _resident(wa.shape), _resident(wb.shape), _resident(wo.shape),
        ],
        out_specs=rows(d),
        compiler_params=_params(1),
        name="mix_out",
    )(moba_o, sb_o, proj, proj, x2d, wa, wb, wo)


def _mem_kv_kernel(mem_ref, g_ref, w_ref, o_ref):
    h = _rms_scale(mem_ref[0], g_ref[...]).astype(BF16)
    o_ref[0] = _dot(h, w_ref[...]).astype(o_ref.dtype)


def _mem_kv(mem, g, w):
    b, m, d = mem.shape
    return pl.pallas_call(
        _mem_kv_kernel,
        out_shape=jax.ShapeDtypeStruct((b, m, w.shape[1]), BF16),
        grid=(b,),
        in_specs=[pl.BlockSpec((1, m, d), lambda i: (i, 0, 0)), _resident((1, d)), _resident(w.shape)],
        out_specs=pl.BlockSpec((1, m, w.shape[1]), lambda i: (i, 0, 0)),
        compiler_params=_params(1),
        name="mem_kv",
    )(mem, g, w)


def _cross_kernel(x_ref, kv_ref, g_ref, wq_ref, wo_ref, o_ref):
    x = x_ref[0]
    h = _rms_scale(x, g_ref[...]).astype(BF16)
    q = (_dot(h, wq_ref[...]) * (X_HEAD_DIM ** -0.5)).astype(BF16)
    heads = []
    for hd in range(X_HEADS):
        cols = slice(hd * X_HEAD_DIM, (hd + 1) * X_HEAD_DIM)
        k_h = kv_ref[0, :, cols]
        v_h = kv_ref[0, :, X_WIDTH + hd * X_HEAD_DIM:X_WIDTH + (hd + 1) * X_HEAD_DIM]
        s = _dot_nt(q[:, cols], k_h)
        p = jnp.exp(s - jnp.max(s, axis=1, keepdims=True))
        l = jnp.sum(p, axis=1, keepdims=True)
        heads.append((_dot(p.astype(BF16), v_h) / l).astype(BF16))
    attn = jnp.concatenate(heads, axis=1)
    o_ref[0] = x + _dot(attn, wo_ref[...])


def _cross(x3, kv, g, wq, wo):
    b, seq, d = x3.shape
    m = kv.shape[1]
    return pl.pallas_call(
        _cross_kernel,
        out_shape=jax.ShapeDtypeStruct((b, seq, d), F32),
        grid=(b, seq // ROW_TILE),
        in_specs=[
            pl.BlockSpec((1, ROW_TILE, d), lambda bi, i: (bi, i, 0)),
            pl.BlockSpec((1, m, kv.shape[2]), lambda bi, i: (bi, 0, 0)),
            _resident((1, d)), _resident(wq.shape), _resident(wo.shape),
        ],
        out_specs=pl.BlockSpec((1, ROW_TILE, d), lambda bi, i: (bi, i, 0)),
        compiler_params=_params(2),
        name="cross_attn",
    )(x3, kv, g, wq, wo)


def _mlp_kernel(x_ref, g_ref, wu_ref, wd_ref, gf_ref, o_ref, *, final_norm):
    x = x_ref[...]
    h = _rms_scale(x, g_ref[...]).astype(BF16)
    acc = x
    for c in range(wu_ref.shape[1] // COL_CHUNK):
        cols = slice(c * COL_CHUNK, (c + 1) * COL_CHUNK)
        u = jnp.maximum(_dot(h, wu_ref[:, cols]), 0.0)
        acc = acc + _dot((u * u).astype(BF16), wd_ref[cols, :])
    o_ref[...] = _rms_scale(acc, gf_ref[...]) if final_norm else acc


def _mlp(x2d, g, wu, wd, gf, final_norm):
    n, d = x2d.shape
    return pl.pallas_call(
        functools.partial(_mlp_kernel, final_norm=final_norm),
        out_shape=jax.ShapeDtypeStruct((n, d), F32),
        grid=(n // ROW_TILE,),
        in_specs=[
            pl.BlockSpec((ROW_TILE, d), lambda i: (i, 0)),
            _resident((1, d)), _resident(wu.shape), _resident(wd.shape), _resident((1, d)),
        ],
        out_specs=pl.BlockSpec((ROW_TILE, d), lambda i: (i, 0)),
        compiler_params=_params(1),
        name="mlp_final",
    )(x2d, g, wu, wd, gf)


def kernel(x, mem, g_mix, w_in, w_br_moba, w_br_sb, w_out, g_cross, g_mem,
           w_xq, w_xkv, w_xo, g_mlp, w_up, w_down, g_final):
    b, seq, d = x.shape
    n = b * seq
    assert n % ROW_TILE == 0 and seq % ROW_TILE == 0
    row = lambda v: v.reshape(1, -1).astype(F32)
    depth = g_mix.shape[0]
    x2d = x.reshape(n, d)
    for l in range(depth):
        proj = _in_proj(x2d, row(g_mix[l]), w_in[l].astype(BF16))
        proj3 = proj.reshape(b, seq, -1)
        moba_o = _moba(proj3, 0, MIX_WIDTH, 2 * MIX_WIDTH)
        sb_o = _stick_breaking(proj3, 3 * MIX_WIDTH, 4 * MIX_WIDTH, 5 * MIX_WIDTH)
        x2d = _mix_out(moba_o.reshape(n, -1), sb_o.reshape(n, -1), proj, 6 * MIX_WIDTH, x2d,
                       w_br_moba[l].astype(BF16), w_br_sb[l].astype(BF16), w_out[l].astype(BF16))
        kv = _mem_kv(mem, row(g_mem[l]), w_xkv[l].astype(BF16))
        x3 = _cross(x2d.reshape(b, seq, d), kv, row(g_cross[l]),
                    w_xq[l].astype(BF16), w_xo[l].astype(BF16))
        x2d = _mlp(x3.reshape(n, d), row(g_mlp[l]), w_up[l].astype(BF16), w_down[l].astype(BF16),
                   row(g_final), final_norm=(l == depth - 1))
    return x2d.reshape(b, seq, d)
```

```python
import functools

import jax
import jax.numpy as jnp
from jax import lax
from jax.experimental import pallas as pl
from jax.experimental.pallas import tpu as pltpu

F32 = jnp.float32
BF16 = jnp.bfloat16

HEAD_DIM = 64
MOBA_HEADS = 8
SB_HEADS = 8
MIX_WIDTH = MOBA_HEADS * HEAD_DIM
MOBA_BLOCK = 256
MOBA_TOPK = 3
X_HEADS = 4
X_HEAD_DIM = 128
X_WIDTH = X_HEADS * X_HEAD_DIM
RMS_EPS = 1e-6
NEG = -1e30

LANES = 128
SUBLANES = 8
HEADS_PER_TILE = LANES // HEAD_DIM
ATT_BLOCK = 256
ROW_TILE = 512
COL_CHUNK = 512
VMEM_LIMIT = 56 * 1024 * 1024


def _params(n_axes):
    return pltpu.CompilerParams(
        dimension_semantics=("arbitrary",) * n_axes, vmem_limit_bytes=VMEM_LIMIT)


def _resident(shape):
    return pl.BlockSpec(shape, lambda *_: (0,) * len(shape), pipeline_mode=pl.Buffered(1))


def _rms_scale(x, g):
    y = x * lax.rsqrt(jnp.mean(x * x, axis=-1, keepdims=True) + RMS_EPS)
    return y * g


def _dot(a, b):
    return jnp.dot(a, b, preferred_element_type=F32)


def _dot_nt(a, b):
    return lax.dot_general(a, b, (((1,), (1,)), ((), ())), preferred_element_type=F32)


def _in_proj_kernel(x_ref, g_ref, w_ref, o_ref, *, q_chunks, gate_start):
    h = _rms_scale(x_ref[...], g_ref[...]).astype(BF16)
    n_chunks = w_ref.shape[1] // COL_CHUNK
    for c in range(n_chunks):
        cols = slice(c * COL_CHUNK, (c + 1) * COL_CHUNK)
        acc = _dot(h, w_ref[:, cols])
        if c in q_chunks:
            acc = acc * (HEAD_DIM ** -0.5)
        if c >= gate_start:
            acc = 1.0 / (1.0 + jnp.exp(-acc))
        o_ref[:, cols] = acc.astype(o_ref.dtype)


def _in_proj(x2d, g, w):
    n, d = x2d.shape
    cols = w.shape[1]
    q_chunks = (0, 3 * MIX_WIDTH // COL_CHUNK)
    gate_start = 6 * MIX_WIDTH // COL_CHUNK
    return pl.pallas_call(
        functools.partial(_in_proj_kernel, q_chunks=q_chunks, gate_start=gate_start),
        out_shape=jax.ShapeDtypeStruct((n, cols), BF16),
        grid=(n // ROW_TILE,),
        in_specs=[
            pl.BlockSpec((ROW_TILE, d), lambda i: (i, 0)),
            _resident((1, d)),
            _resident((d, cols)),
        ],
        out_specs=pl.BlockSpec((ROW_TILE, cols), lambda i: (i, 0)),
        compiler_params=_params(1),
        name="in_proj",
    )(x2d, g, w)


MOBA_SKEW = 2


def _moba_kernel(q_ref, k_ref, v_ref, kaux_ref, qaux_ref, o_ref,
                 kaug_ref, vaug_ref, km_ref):
    seq = q_ref.shape[1]
    nb = seq // ATT_BLOCK
    lane = lax.broadcasted_iota(jnp.int32, (1, LANES), 1)
    row_t = lax.broadcasted_iota(jnp.int32, (ATT_BLOCK, ATT_BLOCK), 0)
    col_t = lax.broadcasted_iota(jnp.int32, (ATT_BLOCK, ATT_BLOCK), 1)
    causal = col_t <= row_t
    lane_q = lax.broadcasted_iota(jnp.int32, (ATT_BLOCK, LANES), 1)
    nb_rows = km_ref.shape[1]
    blk_row = lax.broadcasted_iota(jnp.int32, (SUBLANES, ATT_BLOCK), 0)
    sel_row = lax.broadcasted_iota(jnp.int32, (SUBLANES, LANES), 0)
    sel_lane = lax.broadcasted_iota(jnp.int32, (SUBLANES, LANES), 1)

    k_all = k_ref[0]
    v_all = v_ref[0]
    means = [jnp.mean(k_all[j * ATT_BLOCK:(j + 1) * ATT_BLOCK].astype(F32), axis=0, keepdims=True)
             for j in range(nb)]
    means.append(jnp.zeros((nb_rows - nb, LANES), F32))
    means = jnp.concatenate(means, axis=0)
    for hh in range(HEADS_PER_TILE):
        own = (lane >= hh * HEAD_DIM) & (lane < (hh + 1) * HEAD_DIM)
        kaug_ref[hh] = jnp.where(own, k_all, kaux_ref[hh])
        vaug_ref[hh] = jnp.where(own, v_all, jnp.ones_like(v_all))
        km_ref[hh] = jnp.where(own, means, 0.0).astype(BF16)

    units = [(i, hh) for i in range(nb) for hh in range(HEADS_PER_TILE)]
    n = len(units)
    q_aug, tiles, row_max, acc, outs = {}, {}, {}, {}, {}

    def prep(u):
        i, hh = units[u]
        q_i = q_ref[0, i * ATT_BLOCK:(i + 1) * ATT_BLOCK, :]
        own_q = (lane_q >= hh * HEAD_DIM) & (lane_q < (hh + 1) * HEAD_DIM)
        spare0 = (1 - hh) * HEAD_DIM
        jidx = lane_q - spare0
        if i > MOBA_TOPK:
            gate = _dot_nt(km_ref[hh], q_i)[:SUBLANES]
            beaten = jnp.zeros((SUBLANES, ATT_BLOCK), F32)
            for jp in range(i):
                g_jp = gate[jp:jp + 1, :]
                wins = (g_jp > gate) | ((g_jp == gate) & (jp < blk_row))
                beaten = beaten + jnp.where(wins, 1.0, 0.0)
            attend = ((blk_row < i) & (beaten < MOBA_TOPK)) | (blk_row == i)
            pen_t = jnp.where(attend | (blk_row >= nb), 0.0, NEG).astype(BF16)
            place = jnp.where(sel_lane == sel_row + spare0, 1.0, 0.0).astype(BF16)
            pen = lax.dot_general(pen_t, place, (((0,), (0,)), ((), ())),
                                  preferred_element_type=F32)
            spare_vals = pen + qaux_ref[0, hh:hh + 1, :]
        else:
            attend = (jidx >= 0) & (jidx <= i)
            is_block_lane = (jidx >= 0) & (jidx < nb)
            spare_vals = jnp.where(is_block_lane, jnp.where(attend, 0.0, NEG),
                                   qaux_ref[0, hh:hh + 1, :])
        q_aug[u] = jnp.where(own_q, q_i, spare_vals.astype(BF16))

    def qk(u):
        i, hh = units[u]
        ts = []
        for j in range(i + 1):
            s = _dot_nt(q_aug[u], kaug_ref[hh, j * ATT_BLOCK:(j + 1) * ATT_BLOCK, :])
            if j == i:
                s = jnp.where(causal, s, NEG)
            ts.append(s)
        m_el = ts[0]
        for s in ts[1:]:
            m_el = jnp.maximum(m_el, s)
        tiles[u] = ts
        row_max[u] = jnp.max(m_el, axis=1, keepdims=True)

    def pv(u):
        i, hh = units[u]
        p = jnp.concatenate([jnp.exp(s - row_max[u]).astype(BF16) for s in tiles[u]], axis=1)
        acc[u] = _dot(p, vaug_ref[hh, 0:(i + 1) * ATT_BLOCK, :])

    def fin(u):
        i, hh = units[u]
        denom = pltpu.roll(acc[u], HEAD_DIM, axis=1)
        outs[u] = acc[u] / denom
        if hh == HEADS_PER_TILE - 1:
            o_pair = jnp.where(lane_q < HEAD_DIM, outs[u - 1], outs[u])
            o_ref[0, i * ATT_BLOCK:(i + 1) * ATT_BLOCK, :] = o_pair.astype(o_ref.dtype)

    stages = (prep, qk, pv, fin)
    for t in range(n + (len(stages) - 1) * MOBA_SKEW):
        for k, stage in enumerate(stages):
            u = t - k * MOBA_SKEW
            if 0 <= u < n:
                stage(u)


def _moba_aux(seq):
    nb = seq // MOBA_BLOCK
    pos = jnp.arange(seq)
    lane = jnp.arange(LANES)
    kaux, qaux = [], []
    for hh in range(HEADS_PER_TILE):
        spare0 = (1 - hh) * HEAD_DIM
        j = lane[None, :] - spare0
        ind = (j == (pos // MOBA_BLOCK)[:, None]).astype(F32)
        off = jnp.where(j == nb, (pos % MOBA_BLOCK)[:, None], 0).astype(F32)
        start = jnp.where(j == nb + 1, ((pos // MOBA_BLOCK) * MOBA_BLOCK)[:, None], 0).astype(F32)
        kaux.append(ind + off + start)
        qaux.append(((j == nb) | (j == nb + 1))[0].astype(F32))
    kaux = jnp.stack(kaux).astype(BF16)
    qaux = jnp.stack(qaux)
    slopes = jnp.asarray([2.0 ** (-8.0 * (h + 1) / MOBA_HEADS) for h in range(MOBA_HEADS)], F32)
    slopes = slopes.reshape(MOBA_HEADS // HEADS_PER_TILE, HEADS_PER_TILE, 1)
    return kaux, qaux[None] * slopes


def _moba(proj3, q_col, k_col, v_col):
    b, seq, _ = proj3.shape
    nb = seq // MOBA_BLOCK
    assert seq % MOBA_BLOCK == 0 and nb <= SUBLANES
    pairs = MOBA_HEADS // HEADS_PER_TILE
    kaux, qaux = _moba_aux(seq)
    blk = lambda c0: pl.BlockSpec((1, seq, LANES), lambda bi, p: (bi, 0, c0 + p))
    return pl.pallas_call(
        _moba_kernel,
        out_shape=jax.ShapeDtypeStruct((b, seq, MIX_WIDTH), BF16),
        grid=(b, pairs),
        in_specs=[
            blk(q_col // LANES), blk(k_col // LANES), blk(v_col // LANES),
            _resident((HEADS_PER_TILE, seq, LANES)),
            pl.BlockSpec((1, HEADS_PER_TILE, LANES), lambda bi, p: (p, 0, 0)),
        ],
        out_specs=pl.BlockSpec((1, seq, LANES), lambda bi, p: (bi, 0, p)),
        scratch_shapes=[
            pltpu.VMEM((HEADS_PER_TILE, seq, LANES), BF16),
            pltpu.VMEM((HEADS_PER_TILE, seq, LANES), BF16),
            pltpu.VMEM((HEADS_PER_TILE, 2 * SUBLANES * (-(-nb // (2 * SUBLANES))), LANES), BF16),
        ],
        compiler_params=_params(2),
        name="moba",
    )(proj3, proj3, proj3, kaux, qaux)


SB_DEAD = 120.0
SB_SKEW = 2


def _sb_kernel(q_ref, k_ref, v_ref, tri_ref, o_ref, acc_ref, rest_ref):
    seq = q_ref.shape[1]
    nb = seq // ATT_BLOCK
    row_t = lax.broadcasted_iota(jnp.int32, (ATT_BLOCK, ATT_BLOCK), 0)
    col_t = lax.broadcasted_iota(jnp.int32, (ATT_BLOCK, ATT_BLOCK), 1)
    strict = col_t < row_t
    lane_q = lax.broadcasted_iota(jnp.int32, (ATT_BLOCK, LANES), 1)
    tri = tri_ref[...]
    key_rows = lambda j: slice(j * ATT_BLOCK, (j + 1) * ATT_BLOCK)

    def q_own(i, hh):
        q_i = q_ref[0, i * ATT_BLOCK:(i + 1) * ATT_BLOCK, :]
        own_q = (lane_q >= hh * HEAD_DIM) & (lane_q < (hh + 1) * HEAD_DIM)
        return jnp.where(own_q, q_i, jnp.zeros_like(q_i))

    def run_chains(chains, rests):
        n = len(chains)
        z, sp_split, suffix, a = {}, {}, {}, {}
        outs, new_rests = [None] * n, [None] * n

        def scores(c):
            i, hh, j = chains[c]
            z[c] = _dot_nt(q_own(i, hh), k_ref[0, key_rows(j), :])

        def softplus(c):
            i, hh, j = chains[c]
            neg_abs = lax.bitcast_convert_type(
                lax.bitcast_convert_type(z[c], jnp.uint32) | jnp.uint32(0x80000000), F32)
            sp = jnp.maximum(z[c], 0.0) + jnp.log(1.0 + jnp.exp(neg_abs))
            if j == i:
                sp = jnp.where(strict, sp, 0.0)
            hi = sp.astype(BF16)
            lo = (sp - hi.astype(F32)).astype(BF16)
            sp_split[c] = jnp.concatenate([hi, lo], axis=1)

        def suffix_sum(c):
            suffix[c] = _dot(sp_split[c], tri)

        def weights(c):
            i, hh, j = chains[c]
            w = jnp.exp(z[c] - suffix[c] - rests[c])
            if j == i:
                w = jnp.where(strict, w, 0.0)
            a[c] = w.astype(BF16)
            new_rests[c] = rests[c] + suffix[c][:, 0:1]

        def values(c):
            i, hh, j = chains[c]
            outs[c] = _dot(a[c], v_ref[0, key_rows(j), :])

        stages = (scores, softplus, suffix_sum, weights, values)
        for t in range(n + (len(stages) - 1) * SB_SKEW):
            for k, stage in enumerate(stages):
                c = t - k * SB_SKEW
                if 0 <= c < n:
                    stage(c)
        return outs, new_rests

    slot = lambda i, hh: i * HEADS_PER_TILE + hh
    units = lambda d: [(i, hh, i - d) for i in range(d, nb) for hh in range(HEADS_PER_TILE)]

    diag = units(0)
    acc, rest = run_chains(diag, [jnp.zeros((ATT_BLOCK, 1), F32)] * len(diag))
    acc = {(i, hh): o for (i, hh, _), o in zip(diag, acc)}
    rest = {(i, hh): r for (i, hh, _), r in zip(diag, rest)}
    if nb > 1:
        sub = units(1)
        outs, rests = run_chains(sub, [rest[(i, hh)] for (i, hh, _) in sub])
        for (i, hh, _), o, r in zip(sub, outs, rests):
            acc[(i, hh)] = acc[(i, hh)] + o
            rest[(i, hh)] = r
    for (i, hh), o in acc.items():
        acc_ref[slot(i, hh)] = o
        rest_ref[slot(i, hh)] = jnp.broadcast_to(rest[(i, hh)], (ATT_BLOCK, LANES))

    for d in range(2, nb):
        far = units(d)
        low = rest_ref[slot(d, 0)]
        for s in range(slot(d, 0) + 1, nb * HEADS_PER_TILE):
            low = jnp.minimum(low, rest_ref[s])

        @pl.when(jnp.min(low) < SB_DEAD)
        def _():
            outs, rests = run_chains(far, [rest_ref[slot(i, hh)][:, 0:1] for (i, hh, _) in far])
            for (i, hh, _), o, r in zip(far, outs, rests):
                acc_ref[slot(i, hh)] = acc_ref[slot(i, hh)] + o
                rest_ref[slot(i, hh)] = jnp.broadcast_to(r, (ATT_BLOCK, LANES))

    for i in range(nb):
        o_pair = jnp.where(lane_q < HEAD_DIM, acc_ref[slot(i, 0)], acc_ref[slot(i, 1)])
        o_ref[0, i * ATT_BLOCK:(i + 1) * ATT_BLOCK, :] = o_pair.astype(o_ref.dtype)


def _stick_breaking(proj3, q_col, k_col, v_col):
    b, seq, _ = proj3.shape
    assert seq % ATT_BLOCK == 0
    nb = seq // ATT_BLOCK
    pairs = SB_HEADS // HEADS_PER_TILE
    idx = jnp.arange(ATT_BLOCK)
    tri = (idx[:, None] >= idx[None, :]).astype(BF16)
    tri = jnp.concatenate([tri, tri], axis=0)
    blk = lambda c0: pl.BlockSpec((1, seq, LANES), lambda bi, p: (bi, 0, c0 + p))
    state = pltpu.VMEM((nb * HEADS_PER_TILE, ATT_BLOCK, LANES), F32)
    return pl.pallas_call(
        _sb_kernel,
        out_shape=jax.ShapeDtypeStruct((b, seq, MIX_WIDTH), BF16),
        grid=(b, pairs),
        in_specs=[
            blk(q_col // LANES), blk(k_col // LANES), blk(v_col // LANES),
            _resident(tri.shape),
        ],
        out_specs=pl.BlockSpec((1, seq, LANES), lambda bi, p: (bi, 0, p)),
        scratch_shapes=[state, state],
        compiler_params=_params(2),
        name="stick_breaking",
    )(proj3, proj3, proj3, tri)


def _mix_out_kernel(a_ref, b_ref, ga_ref, gb_ref, x_ref, wa_ref, wb_ref, wo_ref, o_ref):
    ya = _dot(a_ref[...], wa_ref[...])
    yb = _dot(b_ref[...], wb_ref[...])
    merged = ga_ref[...].astype(F32) * ya + gb_ref[...].astype(F32) * yb
    o_ref[...] = x_ref[...] + _dot(merged.astype(BF16), wo_ref[...])


def _mix_out(moba_o, sb_o, proj, gate_col, x2d, wa, wb, wo):
    n, d = x2d.shape
    rows = lambda w, c=0: pl.BlockSpec((ROW_TILE, w), lambda i: (i, c))
    return pl.pallas_call(
        _mix_out_kernel,
        out_shape=jax.ShapeDtypeStruct((n, d), F32),
        grid=(n // ROW_TILE,),
        in_specs=[
            rows(MIX_WIDTH), rows(MIX_WIDTH),
            rows(d, gate_col // d), rows(d, gate_col // d + 1),
            rows(d),
            _resident(wa.shape), _resident(wb.shape), _resident(wo.shape),
        ],
        out_specs=rows(d),
        compiler_params=_params(1),
        name="mix_out",
    )(moba_o, sb_o, proj, proj, x2d, wa, wb, wo)


def _mem_kv_kernel(mem_ref, g_ref, w_ref, o_ref):
    h = _rms_scale(mem_ref[0], g_ref[...]).astype(BF16)
    o_ref[0] = _dot(h, w_ref[...]).astype(o_ref.dtype)


def _mem_kv(mem, g, w):
    b, m, d = mem.shape
    return pl.pallas_call(
        _mem_kv_kernel,
        out_shape=jax.ShapeDtypeStruct((b, m, w.shape[1]), BF16),
        grid=(b,),
        in_specs=[pl.BlockSpec((1, m, d), lambda i: (i, 0, 0)), _resident((1, d)), _resident(w.shape)],
        out_specs=pl.BlockSpec((1, m, w.shape[1]), lambda i: (i, 0, 0)),
        compiler_params=_params(1),
        name="mem_kv",
    )(mem, g, w)


def _cross_kernel(x_ref, kv_ref, g_ref, wq_ref, wo_ref, o_ref):
    x = x_ref[0]
    h = _rms_scale(x, g_ref[...]).astype(BF16)
    q = (_dot(h, wq_ref[...]) * (X_HEAD_DIM ** -0.5)).astype(BF16)
    heads = []
    for hd in range(X_HEADS):
        cols = slice(hd * X_HEAD_DIM, (hd + 1) * X_HEAD_DIM)
        k_h = kv_ref[0, :, cols]
        v_h = kv_ref[0, :, X_WIDTH + hd * X_HEAD_DIM:X_WIDTH + (hd + 1) * X_HEAD_DIM]
        s = _dot_nt(q[:, cols], k_h)
        p = jnp.exp(s - jnp.max(s, axis=1, keepdims=True))
        l = jnp.sum(p, axis=1, keepdims=True)
        heads.append((_dot(p.astype(BF16), v_h) / l).astype(BF16))
    attn = jnp.concatenate(heads, axis=1)
    o_ref[0] = x + _dot(attn, wo_ref[...])


def _cross(x3, kv, g, wq, wo):
    b, seq, d = x3.shape
    m = kv.shape[1]
    return pl.pallas_call(
        _cross_kernel,
        out_shape=jax.ShapeDtypeStruct((b, seq, d), F32),
        grid=(b, seq // ROW_TILE),
        in_specs=[
            pl.BlockSpec((1, ROW_TILE, d), lambda bi, i: (bi, i, 0)),
            pl.BlockSpec((1, m, kv.shape[2]), lambda bi, i: (bi, 0, 0)),
            _resident((1, d)), _resident(wq.shape), _resident(wo.shape),
        ],
        out_specs=pl.BlockSpec((1, ROW_TILE, d), lambda bi, i: (bi, i, 0)),
        compiler_params=_params(2),
        name="cross_attn",
    )(x3, kv, g, wq, wo)


def _mlp_kernel(x_ref, g_ref, wu_ref, wd_ref, gf_ref, o_ref, *, final_norm):
    x = x_ref[...]
    h = _rms_scale(x, g_ref[...]).astype(BF16)
    acc = x
    for c in range(wu_ref.shape[1] // COL_CHUNK):
        cols = slice(c * COL_CHUNK, (c + 1) * COL_CHUNK)
        u = jnp.maximum(_dot(h, wu_ref[:, cols]), 0.0)
        acc = acc + _dot((u * u).astype(BF16), wd_ref[cols, :])
    o_ref[...] = _rms_scale(acc, gf_ref[...]) if final_norm else acc


def _mlp(x2d, g, wu, wd, gf, final_norm):
    n, d = x2d.shape
    return pl.pallas_call(
        functools.partial(_mlp_kernel, final_norm=final_norm),
        out_shape=jax.ShapeDtypeStruct((n, d), F32),
        grid=(n // ROW_TILE,),
        in_specs=[
            pl.BlockSpec((ROW_TILE, d), lambda i: (i, 0)),
            _resident((1, d)), _resident(wu.shape), _resident(wd.shape), _resident((1, d)),
        ],
        out_specs=pl.BlockSpec((ROW_TILE, d), lambda i: (i, 0)),
        compiler_params=_params(1),
        name="mlp_final",
    )(x2d, g, wu, wd, gf)


def kernel(x, mem, g_mix, w_in, w_br_moba, w_br_sb, w_out, g_cross, g_mem,
           w_xq, w_xkv, w_xo, g_mlp, w_up, w_down, g_final):
    b, seq, d = x.shape
    n = b * seq
    assert n % ROW_TILE == 0 and seq % ROW_TILE == 0
    row = lambda v: v.reshape(1, -1).astype(F32)
    depth = g_mix.shape[0]
    x2d = x.reshape(n, d)
    for l in range(depth):
        proj = _in_proj(x2d, row(g_mix[l]), w_in[l].astype(BF16))
        proj3 = proj.reshape(b, seq, -1)
        moba_o = _moba(proj3, 0, MIX_WIDTH, 2 * MIX_WIDTH)
        sb_o = _stick_breaking(proj3, 3 * MIX_WIDTH, 4 * MIX_WIDTH, 5 * MIX_WIDTH)
        x2d = _mix_out(moba_o.reshape(n, -1), sb_o.reshape(n, -1), proj, 6 * MIX_WIDTH, x2d,
                       w_br_moba[l].astype(BF16), w_br_sb[l].astype(BF16), w_out[l].astype(BF16))
        kv = _mem_kv(mem, row(g_mem[l]), w_xkv[l].astype(BF16))
        x3 = _cross(x2d.reshape(b, seq, d), kv, row(g_cross[l]),
                    w_xq[l].astype(BF16), w_xo[l].astype(BF16))
        x2d = _mlp(x3.reshape(n, d), row(g_mlp[l]), w_up[l].astype(BF16), w_down[l].astype(BF16),
                   row(g_final), final_norm=(l == depth - 1))
    return x2d.reshape(b, seq, d)
```

```python
import functools

import jax
import jax.numpy as jnp
from jax import lax
from jax.experimental import pallas as pl
from jax.experimental.pallas import tpu as pltpu

F32 = jnp.float32
BF16 = jnp.bfloat16

HEAD_DIM = 64
MOBA_HEADS = 8
SB_HEADS = 8
MIX_WIDTH = MOBA_HEADS * HEAD_DIM
MOBA_BLOCK = 256
MOBA_TOPK = 3
X_HEADS = 4
X_HEAD_DIM = 128
X_WIDTH = X_HEADS * X_HEAD_DIM
RMS_EPS = 1e-6
NEG = -1e30

LANES = 128
SUBLANES = 8
HEADS_PER_TILE = LANES // HEAD_DIM
ATT_BLOCK = 256
ROW_TILE = 512
COL_CHUNK = 512
VMEM_LIMIT = 56 * 1024 * 1024


def _params(n_axes):
    return pltpu.CompilerParams(
        dimension_semantics=("arbitrary",) * n_axes, vmem_limit_bytes=VMEM_LIMIT)


def _resident(shape):
    return pl.BlockSpec(shape, lambda *_: (0,) * len(shape), pipeline_mode=pl.Buffered(1))


def _rms_scale(x, g):
    y = x * lax.rsqrt(jnp.mean(x * x, axis=-1, keepdims=True) + RMS_EPS)
    return y * g


def _dot(a, b):
    return jnp.dot(a, b, preferred_element_type=F32)


def _dot_nt(a, b):
    return lax.dot_general(a, b, (((1,), (1,)), ((), ())), preferred_element_type=F32)


def _in_proj_kernel(x_ref, g_ref, w_ref, o_ref, *, q_chunks, gate_start):
    n_chunks = w_ref.shape[1] // COL_CHUNK
    half = x_ref.shape[0] // 2
    rows = (slice(0, half), slice(half, 2 * half))
    h = {}

    def norm(r):
        h[r] = _rms_scale(x_ref[rows[r], :], g_ref[...]).astype(BF16)

    def chunk(r, c):
        cols = slice(c * COL_CHUNK, (c + 1) * COL_CHUNK)
        acc = _dot(h[r], w_ref[:, cols])
        if c in q_chunks:
            acc = acc * (HEAD_DIM ** -0.5)
        if c >= gate_start:
            acc = 1.0 / (1.0 + jnp.exp(-acc))
        o_ref[rows[r], cols] = acc.astype(o_ref.dtype)

    norm(0)
    chunk(0, 0)
    norm(1)
    for c in range(1, n_chunks):
        chunk(1, c - 1)
        chunk(0, c)
    chunk(1, n_chunks - 1)


def _in_proj(x2d, g, w):
    n, d = x2d.shape
    cols = w.shape[1]
    q_chunks = (0, 3 * MIX_WIDTH // COL_CHUNK)
    gate_start = 6 * MIX_WIDTH // COL_CHUNK
    return pl.pallas_call(
        functools.partial(_in_proj_kernel, q_chunks=q_chunks, gate_start=gate_start),
        out_shape=jax.ShapeDtypeStruct((n, cols), BF16),
        grid=(n // ROW_TILE,),
        in_specs=[
            pl.BlockSpec((ROW_TILE, d), lambda i: (i, 0)),
            _resident((1, d)),
            _resident((d, cols)),
        ],
        out_specs=pl.BlockSpec((ROW_TILE, cols), lambda i: (i, 0)),
        compiler_params=_params(1),
        name="in_proj",
    )(x2d, g, w)


MOBA_SKEW = 1


def _moba_kernel(q_ref, k_ref, v_ref, kaux_ref, qaux_ref, o_ref,
                 kaug_ref, vaug_ref, km_ref):
    seq = q_ref.shape[1]
    nb = seq // ATT_BLOCK
    lane = lax.broadcasted_iota(jnp.int32, (1, LANES), 1)
    row_t = lax.broadcasted_iota(jnp.int32, (ATT_BLOCK, ATT_BLOCK), 0)
    col_t = lax.broadcasted_iota(jnp.int32, (ATT_BLOCK, ATT_BLOCK), 1)
    causal = col_t <= row_t
    lane_q = lax.broadcasted_iota(jnp.int32, (ATT_BLOCK, LANES), 1)
    nb_rows = km_ref.shape[1]
    blk_row = lax.broadcasted_iota(jnp.int32, (SUBLANES, ATT_BLOCK), 0)
    sel_row = lax.broadcasted_iota(jnp.int32, (SUBLANES, LANES), 0)
    sel_lane = lax.broadcasted_iota(jnp.int32, (SUBLANES, LANES), 1)

    k_all = k_ref[0]
    v_all = v_ref[0]
    means = [jnp.mean(k_all[j * ATT_BLOCK:(j + 1) * ATT_BLOCK].astype(F32), axis=0, keepdims=True)
             for j in range(nb)]
    means.append(jnp.zeros((nb_rows - nb, LANES), F32))
    means = jnp.concatenate(means, axis=0)
    for hh in range(HEADS_PER_TILE):
        own = (lane >= hh * HEAD_DIM) & (lane < (hh + 1) * HEAD_DIM)
        kaug_ref[hh] = jnp.where(own, k_all, kaux_ref[hh])
        vaug_ref[hh] = jnp.where(own, v_all, jnp.ones_like(v_all))
        km_ref[hh] = jnp.where(own, means, 0.0).astype(BF16)

    units = [(i, hh) for i in range(nb) for hh in range(HEADS_PER_TILE)]
    n = len(units)
    q_aug, tiles, row_max, probs, acc, outs = {}, {}, {}, {}, {}, {}

    def prep(u):
        i, hh = units[u]
        q_i = q_ref[0, i * ATT_BLOCK:(i + 1) * ATT_BLOCK, :]
        own_q = (lane_q >= hh * HEAD_DIM) & (lane_q < (hh + 1) * HEAD_DIM)
        spare0 = (1 - hh) * HEAD_DIM
        jidx = lane_q - spare0
        if i > MOBA_TOPK:
            gate = _dot_nt(km_ref[hh], q_i)[:SUBLANES]
            beaten = jnp.zeros((SUBLANES, ATT_BLOCK), F32)
            for jp in range(i):
                g_jp = gate[jp:jp + 1, :]
                wins = (g_jp > gate) | ((g_jp == gate) & (jp < blk_row))
                beaten = beaten + jnp.where(wins, 1.0, 0.0)
            attend = ((blk_row < i) & (beaten < MOBA_TOPK)) | (blk_row == i)
            pen_t = jnp.where(attend | (blk_row >= nb), 0.0, NEG).astype(BF16)
            place = jnp.where(sel_lane == sel_row + spare0, 1.0, 0.0).astype(BF16)
            pen = lax.dot_general(pen_t, place, (((0,), (0,)), ((), ())),
                                  preferred_element_type=F32)
            spare_vals = pen + qaux_ref[0, hh:hh + 1, :]
        else:
            attend = (jidx >= 0) & (jidx <= i)
            is_block_lane = (jidx >= 0) & (jidx < nb)
            spare_vals = jnp.where(is_block_lane, jnp.where(attend, 0.0, NEG),
                                   qaux_ref[0, hh:hh + 1, :])
        q_aug[u] = jnp.where(own_q, q_i, spare_vals.astype(BF16))

    def qk(u):
        i, hh = units[u]
        ts = []
        for j in range(i + 1):
            s = _dot_nt(q_aug[u], kaug_ref[hh, j * ATT_BLOCK:(j + 1) * ATT_BLOCK, :])
            if j == i:
                s = jnp.where(causal, s, NEG)
            ts.append(s)
        m_el = ts[0]
        for s in ts[1:]:
            m_el = jnp.maximum(m_el, s)
        tiles[u] = ts
        row_max[u] = jnp.max(m_el, axis=1, keepdims=True)

    def expo(u):
        probs[u] = jnp.concatenate([jnp.exp(s - row_max[u]).astype(BF16) for s in tiles[u]], axis=1)

    def pv(u):
        i, hh = units[u]
        acc[u] = _dot(probs[u], vaug_ref[hh, 0:(i + 1) * ATT_BLOCK, :])

    def fin(u):
        i, hh = units[u]
        denom = pltpu.roll(acc[u], HEAD_DIM, axis=1)
        outs[u] = acc[u] / denom
        if hh == HEADS_PER_TILE - 1:
            o_pair = jnp.where(lane_q < HEAD_DIM, outs[u - 1], outs[u])
            o_ref[0, i * ATT_BLOCK:(i + 1) * ATT_BLOCK, :] = o_pair.astype(o_ref.dtype)

    stages = (prep, qk, expo, pv, fin)
    for t in range(n + (len(stages) - 1) * MOBA_SKEW):
        for k, stage in enumerate(stages):
            u = t - k * MOBA_SKEW
            if 0 <= u < n:
                stage(u)


def _moba_aux(seq):
    nb = seq // MOBA_BLOCK
    pos = jnp.arange(seq)
    lane = jnp.arange(LANES)
    kaux, qaux = [], []
    for hh in range(HEADS_PER_TILE):
        spare0 = (1 - hh) * HEAD_DIM
        j = lane[None, :] - spare0
        ind = (j == (pos // MOBA_BLOCK)[:, None]).astype(F32)
        off = jnp.where(j == nb, (pos % MOBA_BLOCK)[:, None], 0).astype(F32)
        start = jnp.where(j == nb + 1, ((pos // MOBA_BLOCK) * MOBA_BLOCK)[:, None], 0).astype(F32)
        kaux.append(ind + off + start)
        qaux.append(((j == nb) | (j == nb + 1))[0].astype(F32))
    kaux = jnp.stack(kaux).astype(BF16)
    qaux = jnp.stack(qaux)
    slopes = jnp.asarray([2.0 ** (-8.0 * (h + 1) / MOBA_HEADS) for h in range(MOBA_HEADS)], F32)
    slopes = slopes.reshape(MOBA_HEADS // HEADS_PER_TILE, HEADS_PER_TILE, 1)
    return kaux, qaux[None] * slopes


def _moba(proj3, q_col, k_col, v_col):
    b, seq, _ = proj3.shape
    nb = seq // MOBA_BLOCK
    assert seq % MOBA_BLOCK == 0 and nb <= SUBLANES
    pairs = MOBA_HEADS // HEADS_PER_TILE
    kaux, qaux = _moba_aux(seq)
    blk = lambda c0: pl.BlockSpec((1, seq, LANES), lambda bi, p: (bi, 0, c0 + p))
    return pl.pallas_call(
        _moba_kernel,
        out_shape=jax.ShapeDtypeStruct((b, seq, MIX_WIDTH), BF16),
        grid=(b, pairs),
        in_specs=[
            blk(q_col // LANES), blk(k_col // LANES), blk(v_col // LANES),
            _resident((HEADS_PER_TILE, seq, LANES)),
            pl.BlockSpec((1, HEADS_PER_TILE, LANES), lambda bi, p: (p, 0, 0)),
        ],
        out_specs=pl.BlockSpec((1, seq, LANES), lambda bi, p: (bi, 0, p)),
        scratch_shapes=[
            pltpu.VMEM((HEADS_PER_TILE, seq, LANES), BF16),
            pltpu.VMEM((HEADS_PER_TILE, seq, LANES), BF16),
            pltpu.VMEM((HEADS_PER_TILE, 2 * SUBLANES * (-(-nb // (2 * SUBLANES))), LANES), BF16),
        ],
        compiler_params=_params(2),
        name="moba",
    )(proj3, proj3, proj3, kaux, qaux)


SB_DEAD = 120.0
SB_SKEW = 2


def _sb_kernel(q_ref, k_ref, v_ref, tri_ref, o_ref, acc_ref, rest_ref):
    seq = q_ref.shape[1]
    nb = seq // ATT_BLOCK
    row_t = lax.broadcasted_iota(jnp.int32, (ATT_BLOCK, ATT_BLOCK), 0)
    col_t = lax.broadcasted_iota(jnp.int32, (ATT_BLOCK, ATT_BLOCK), 1)
    strict = col_t < row_t
    lane_q = lax.broadcasted_iota(jnp.int32, (ATT_BLOCK, LANES), 1)
    tri = tri_ref[...]
    key_rows = lambda j: slice(j * ATT_BLOCK, (j + 1) * ATT_BLOCK)

    def q_own(i, hh):
        q_i = q_ref[0, i * ATT_BLOCK:(i + 1) * ATT_BLOCK, :]
        own_q = (lane_q >= hh * HEAD_DIM) & (lane_q < (hh + 1) * HEAD_DIM)
        return jnp.where(own_q, q_i, jnp.zeros_like(q_i))

    def run_chains(chains, rests):
        n = len(chains)
        z, sp_bf, suffix, a = {}, {}, {}, {}
        outs, new_rests = [None] * n, [None] * n

        def scores(c):
            i, hh, j = chains[c]
            z[c] = _dot_nt(q_own(i, hh), k_ref[0, key_rows(j), :])

        def softplus(c):
            i, hh, j = chains[c]
            neg_abs = lax.bitcast_convert_type(
                lax.bitcast_convert_type(z[c], jnp.uint32) | jnp.uint32(0x80000000), F32)
            sp = jnp.maximum(z[c], 0.0) + jnp.log(1.0 + jnp.exp(neg_abs))
            if j == i:
                sp = jnp.where(strict, sp, 0.0)
            sp_bf[c] = sp.astype(BF16)

        def suffix_sum(c):
            suffix[c] = _dot(sp_bf[c], tri)

        def weights(c):
            i, hh, j = chains[c]
            w = jnp.exp(z[c] - suffix[c] - rests[c])
            if j == i:
                w = jnp.where(strict, w, 0.0)
            a[c] = w.astype(BF16)
            new_rests[c] = rests[c] + suffix[c][:, 0:1]

        def values(c):
            i, hh, j = chains[c]
            outs[c] = _dot(a[c], v_ref[0, key_rows(j), :])

        stages = (scores, softplus, suffix_sum, weights, values)
        for t in range(n + (len(stages) - 1) * SB_SKEW):
            for k, stage in enumerate(stages):
                c = t - k * SB_SKEW
                if 0 <= c < n:
                    stage(c)
        return outs, new_rests

    slot = lambda i, hh: i * HEADS_PER_TILE + hh
    units = lambda d: [(i, hh, i - d) for i in range(d, nb) for hh in range(HEADS_PER_TILE)]

    diag = units(0)
    acc, rest = run_chains(diag, [jnp.zeros((ATT_BLOCK, 1), F32)] * len(diag))
    acc = {(i, hh): o for (i, hh, _), o in zip(diag, acc)}
    rest = {(i, hh): r for (i, hh, _), r in zip(diag, rest)}
    if nb > 1:
        sub = units(1)
        outs, rests = run_chains(sub, [rest[(i, hh)] for (i, hh, _) in sub])
        for (i, hh, _), o, r in zip(sub, outs, rests):
            acc[(i, hh)] = acc[(i, hh)] + o
            rest[(i, hh)] = r
    for (i, hh), o in acc.items():
        acc_ref[slot(i, hh)] = o
        rest_ref[slot(i, hh)] = jnp.broadcast_to(rest[(i, hh)], (ATT_BLOCK, LANES))

    for d in range(2, nb):
        far = units(d)
        low = rest_ref[slot(d, 0)]
        for s in range(slot(d, 0) + 1, nb * HEADS_PER_TILE):
            low = jnp.minimum(low, rest_ref[s])

        @pl.when(jnp.min(low) < SB_DEAD)
        def _():
            outs, rests = run_chains(far, [rest_ref[slot(i, hh)][:, 0:1] for (i, hh, _) in far])
            for (i, hh, _), o, r in zip(far, outs, rests):
                acc_ref[slot(i, hh)] = acc_ref[slot(i, hh)] + o
                rest_ref[slot(i, hh)] = jnp.broadcast_to(r, (ATT_BLOCK, LANES))

    for i in range(nb):
        o_pair = jnp.where(lane_q < HEAD_DIM, acc_ref[slot(i, 0)], acc_ref[slot(i, 1)])
        o_ref[0, i * ATT_BLOCK:(i + 1) * ATT_BLOCK, :] = o_pair.astype(o_ref.dtype)


def _stick_breaking(proj3, q_col, k_col, v_col):
    b, seq, _ = proj3.shape
    assert seq % ATT_BLOCK == 0
    nb = seq // ATT_BLOCK
    pairs = SB_HEADS // HEADS_PER_TILE
    idx = jnp.arange(ATT_BLOCK)
    tri = (idx[:, None] >= idx[None, :]).astype(BF16)
    blk = lambda c0: pl.BlockSpec((1, seq, LANES), lambda bi, p: (bi, 0, c0 + p))
    state = pltpu.VMEM((nb * HEADS_PER_TILE, ATT_BLOCK, LANES), F32)
    return pl.pallas_call(
        _sb_kernel,
        out_shape=jax.ShapeDtypeStruct((b, seq, MIX_WIDTH), BF16),
        grid=(b, pairs),
        in_specs=[
            blk(q_col // LANES), blk(k_col // LANES), blk(v_col // LANES),
            _resident(tri.shape),
        ],
        out_specs=pl.BlockSpec((1, seq, LANES), lambda bi, p: (bi, 0, p)),
        scratch_shapes=[state, state],
        compiler_params=_params(2),
        name="stick_breaking",
    )(proj3, proj3, proj3, tri)


def _mem_kv_kernel(mem_ref, g_ref, w_ref, o_ref):
    h = _rms_scale(mem_ref[0], g_ref[...]).astype(BF16)
    o_ref[0] = _dot(h, w_ref[...]).astype(o_ref.dtype)


def _mem_kv(mem, g, w):
    b, m, d = mem.shape
    return pl.pallas_call(
        _mem_kv_kernel,
        out_shape=jax.ShapeDtypeStruct((b, m, w.shape[1]), BF16),
        grid=(b,),
        in_specs=[pl.BlockSpec((1, m, d), lambda i: (i, 0, 0)), _resident((1, d)), _resident(w.shape)],
        out_specs=pl.BlockSpec((1, m, w.shape[1]), lambda i: (i, 0, 0)),
        compiler_params=_params(1),
        name="mem_kv",
    )(mem, g, w)


def _mix_cross_kernel(a_ref, b_ref, ga_ref, gb_ref, x_ref, kv_ref, g_ref,
                      wa_ref, wb_ref, wo_ref, wq_ref, wxo_ref, o_ref):
    half = x_ref.shape[1] // 2
    rows = (slice(0, half), slice(half, 2 * half))
    ya, yb, merged, x1, q, probs, denom, attn = ({} for _ in range(8))
    head_cols = lambda hd: slice(hd * X_HEAD_DIM, (hd + 1) * X_HEAD_DIM)

    def branches(r):
        ya[r] = _dot(a_ref[0, rows[r], :], wa_ref[...])
        yb[r] = _dot(b_ref[0, rows[r], :], wb_ref[...])

    def merge(r):
        m = ga_ref[0, rows[r], :].astype(F32) * ya[r] + gb_ref[0, rows[r], :].astype(F32) * yb[r]
        merged[r] = m.astype(BF16)

    def mix_residual(r):
        x1[r] = x_ref[0, rows[r], :] + _dot(merged[r], wo_ref[...])

    def project(r):
        h = _rms_scale(x1[r], g_ref[...]).astype(BF16)
        q[r] = (_dot(h, wq_ref[...]) * (X_HEAD_DIM ** -0.5)).astype(BF16)

    def scores(r):
        probs[r], denom[r] = [], []
        for hd in range(X_HEADS):
            s = _dot_nt(q[r][:, head_cols(hd)], kv_ref[0, :, head_cols(hd)])
            p = jnp.exp(s - jnp.max(s, axis=1, keepdims=True))
            denom[r].append(jnp.sum(p, axis=1, keepdims=True))
            probs[r].append(p.astype(BF16))

    def values(r):
        heads = []
        for hd in range(X_HEADS):
            v_h = kv_ref[0, :, X_WIDTH + hd * X_HEAD_DIM:X_WIDTH + (hd + 1) * X_HEAD_DIM]
            heads.append((_dot(probs[r][hd], v_h) / denom[r][hd]).astype(BF16))
        attn[r] = jnp.concatenate(heads, axis=1)

    def cross_residual(r):
        o_ref[0, rows[r], :] = x1[r] + _dot(attn[r], wxo_ref[...])

    stages = (branches, merge, mix_residual, project, scores, values, cross_residual)
    for t in range(len(rows) + len(stages) - 1):
        for k, stage in enumerate(stages):
            if 0 <= t - k < len(rows):
                stage(t - k)


def _mix_cross(moba_o, sb_o, proj3, gate_col, x3, kv, g, wa, wb, wo, wq, wxo):
    b, seq, d = x3.shape
    m = kv.shape[1]
    rows = lambda w, c=0: pl.BlockSpec((1, ROW_TILE, w), lambda bi, i: (bi, i, c))
    return pl.pallas_call(
        _mix_cross_kernel,
        out_shape=jax.ShapeDtypeStruct((b, seq, d), F32),
        grid=(b, seq // ROW_TILE),
        in_specs=[
            rows(MIX_WIDTH), rows(MIX_WIDTH),
            rows(d, gate_col // d), rows(d, gate_col // d + 1),
            rows(d),
            pl.BlockSpec((1, m, kv.shape[2]), lambda bi, i: (bi, 0, 0)),
            _resident((1, d)),
            _resident(wa.shape), _resident(wb.shape), _resident(wo.shape),
            _resident(wq.shape), _resident(wxo.shape),
        ],
        out_specs=rows(d),
        compiler_params=_params(2),
        name="mix_cross",
    )(moba_o, sb_o, proj3, proj3, x3, kv, g, wa, wb, wo, wq, wxo)


def _mlp_kernel(x_ref, g_ref, wu_ref, wd_ref, gf_ref, o_ref, *, final_norm):
    x = x_ref[...]
    h = _rms_scale(x, g_ref[...]).astype(BF16)
    acc = x
    for c in range(wu_ref.shape[1] // COL_CHUNK):
        cols = slice(c * COL_CHUNK, (c + 1) * COL_CHUNK)
        u = jnp.maximum(_dot(h, wu_ref[:, cols]), 0.0)
        acc = acc + _dot((u * u).astype(BF16), wd_ref[cols, :])
    o_ref[...] = _rms_scale(acc, gf_ref[...]) if final_norm else acc


def _mlp(x2d, g, wu, wd, gf, final_norm):
    n, d = x2d.shape
    return pl.pallas_call(
        functools.partial(_mlp_kernel, final_norm=final_norm),
        out_shape=jax.ShapeDtypeStruct((n, d), F32),
        grid=(n // ROW_TILE,),
        in_specs=[
            pl.BlockSpec((ROW_TILE, d), lambda i: (i, 0)),
            _resident((1, d)), _resident(wu.shape), _resident(wd.shape), _resident((1, d)),
        ],
        out_specs=pl.BlockSpec((ROW_TILE, d), lambda i: (i, 0)),
        compiler_params=_params(1),
        name="mlp_final",
    )(x2d, g, wu, wd, gf)


def kernel(x, mem, g_mix, w_in, w_br_moba, w_br_sb, w_out, g_cross, g_mem,
           w_xq, w_xkv, w_xo, g_mlp, w_up, w_down, g_final):
    b, seq, d = x.shape
    n = b * seq
    assert n % ROW_TILE == 0 and seq % ROW_TILE == 0
    row = lambda v: v.reshape(1, -1).astype(F32)
    depth = g_mix.shape[0]
    x2d = x.reshape(n, d)
    for l in range(depth):
        proj = _in_proj(x2d, row(g_mix[l]), w_in[l].astype(BF16))
        proj3 = proj.reshape(b, seq, -1)
        moba_o = _moba(proj3, 0, MIX_WIDTH, 2 * MIX_WIDTH)
        sb_o = _stick_breaking(proj3, 3 * MIX_WIDTH, 4 * MIX_WIDTH, 5 * MIX_WIDTH)
        kv = _mem_kv(mem, row(g_mem[l]), w_xkv[l].astype(BF16))
        x3 = _mix_cross(moba_o, sb_o, proj3, 6 * MIX_WIDTH, x2d.reshape(b, seq, d), kv, row(g_cross[l]),
                        w_br_moba[l].astype(BF16), w_br_sb[l].astype(BF16), w_out[l].astype(BF16),
                        w_xq[l].astype(BF16), w_xo[l].astype(BF16))
        x2d = _mlp(x3.reshape(n, d), row(g_mlp[l]), w_up[l].astype(BF16), w_down[l].astype(BF16),
                   row(g_final), final_norm=(l == depth - 1))
    return x2d.reshape(b, seq, d)
```

```python
import functools

import jax
import jax.numpy as jnp
from jax import lax
from jax.experimental import pallas as pl
from jax.experimental.pallas import tpu as pltpu

F32 = jnp.float32
BF16 = jnp.bfloat16

HEAD_DIM = 64
MOBA_HEADS = 8
SB_HEADS = 8
MIX_WIDTH = MOBA_HEADS * HEAD_DIM
MOBA_BLOCK = 256
MOBA_TOPK = 3
X_HEADS = 4
X_HEAD_DIM = 128
X_WIDTH = X_HEADS * X_HEAD_DIM
RMS_EPS = 1e-6
NEG = -1e30

LANES = 128
SUBLANES = 8
HEADS_PER_TILE = LANES // HEAD_DIM
ATT_BLOCK = 256
ROW_TILE = 512
COL_CHUNK = 512
VMEM_LIMIT = 56 * 1024 * 1024


def _params(n_axes):
    return pltpu.CompilerParams(
        dimension_semantics=("arbitrary",) * n_axes, vmem_limit_bytes=VMEM_LIMIT)


def _resident(shape):
    return pl.BlockSpec(shape, lambda *_: (0,) * len(shape), pipeline_mode=pl.Buffered(1))


def _rms_scale(x, g):
    y = x * lax.rsqrt(jnp.mean(x * x, axis=-1, keepdims=True) + RMS_EPS)
    return y * g


def _dot(a, b):
    return jnp.dot(a, b, preferred_element_type=F32)


def _dot_nt(a, b):
    return lax.dot_general(a, b, (((1,), (1,)), ((), ())), preferred_element_type=F32)


def _in_proj_kernel(x_ref, g_ref, w_ref, o_ref, *, q_chunks, gate_start):
    n_chunks = w_ref.shape[1] // COL_CHUNK
    half = x_ref.shape[0] // 2
    rows = (slice(0, half), slice(half, 2 * half))
    h = {}

    def norm(r):
        h[r] = _rms_scale(x_ref[rows[r], :], g_ref[...]).astype(BF16)

    def chunk(r, c):
        cols = slice(c * COL_CHUNK, (c + 1) * COL_CHUNK)
        acc = _dot(h[r], w_ref[:, cols])
        if c in q_chunks:
            acc = acc * (HEAD_DIM ** -0.5)
        if c >= gate_start:
            acc = 1.0 / (1.0 + jnp.exp(-acc))
        o_ref[rows[r], cols] = acc.astype(o_ref.dtype)

    norm(0)
    chunk(0, 0)
    norm(1)
    for c in range(1, n_chunks):
        chunk(1, c - 1)
        chunk(0, c)
    chunk(1, n_chunks - 1)


def _in_proj(x2d, g, w):
    n, d = x2d.shape
    cols = w.shape[1]
    q_chunks = (0, 3 * MIX_WIDTH // COL_CHUNK)
    gate_start = 6 * MIX_WIDTH // COL_CHUNK
    return pl.pallas_call(
        functools.partial(_in_proj_kernel, q_chunks=q_chunks, gate_start=gate_start),
        out_shape=jax.ShapeDtypeStruct((n, cols), BF16),
        grid=(n // ROW_TILE,),
        in_specs=[
            pl.BlockSpec((ROW_TILE, d), lambda i: (i, 0)),
            _resident((1, d)),
            _resident((d, cols)),
        ],
        out_specs=pl.BlockSpec((ROW_TILE, cols), lambda i: (i, 0)),
        compiler_params=_params(1),
        name="in_proj",
    )(x2d, g, w)


MOBA_SKEW = 1


def _moba_kernel(q_ref, k_ref, v_ref, kaux_ref, qaux_ref, o_ref,
                 kaug_ref, vaug_t_ref, km_ref):
    seq = q_ref.shape[1]
    nb = seq // ATT_BLOCK
    lane = lax.broadcasted_iota(jnp.int32, (1, LANES), 1)
    key_t = lax.broadcasted_iota(jnp.int32, (ATT_BLOCK, ATT_BLOCK), 0)
    qry_t = lax.broadcasted_iota(jnp.int32, (ATT_BLOCK, ATT_BLOCK), 1)
    causal = key_t <= qry_t
    lane_q = lax.broadcasted_iota(jnp.int32, (ATT_BLOCK, LANES), 1)
    chan_t = lax.broadcasted_iota(jnp.int32, (LANES, ATT_BLOCK), 0)
    nb_rows = km_ref.shape[1]
    blk_row = lax.broadcasted_iota(jnp.int32, (SUBLANES, ATT_BLOCK), 0)

    k_all = k_ref[0]
    v_all = v_ref[0]
    means = [jnp.mean(k_all[j * ATT_BLOCK:(j + 1) * ATT_BLOCK].astype(F32), axis=0, keepdims=True)
             for j in range(nb)]
    means.append(jnp.zeros((nb_rows - nb, LANES), F32))
    means = jnp.concatenate(means, axis=0)
    v_t = v_all.astype(F32).T
    chan_all = lax.broadcasted_iota(jnp.int32, (LANES, seq), 0)
    for hh in range(HEADS_PER_TILE):
        own = (lane >= hh * HEAD_DIM) & (lane < (hh + 1) * HEAD_DIM)
        own_t = (chan_all >= hh * HEAD_DIM) & (chan_all < (hh + 1) * HEAD_DIM)
        kaug_ref[hh] = jnp.where(own, k_all, kaux_ref[hh])
        vaug_t_ref[hh] = jnp.where(own_t, v_t, 1.0).astype(BF16)
        km_ref[hh] = jnp.where(own, means, 0.0).astype(BF16)

    units = [(i, hh) for i in range(nb) for hh in range(HEADS_PER_TILE)]
    n = len(units)
    q_aug, penalty, tiles, col_max, probs, acc, outs = ({} for _ in range(7))

    def prep(u):
        i, hh = units[u]
        q_i = q_ref[0, i * ATT_BLOCK:(i + 1) * ATT_BLOCK, :]
        own_q = (lane_q >= hh * HEAD_DIM) & (lane_q < (hh + 1) * HEAD_DIM)
        q_aug[u] = jnp.where(own_q, q_i, qaux_ref[0, hh:hh + 1, :].astype(BF16))
        if i > MOBA_TOPK:
            gate = _dot_nt(km_ref[hh], q_i)[:SUBLANES]
            beaten = jnp.zeros((SUBLANES, ATT_BLOCK), F32)
            for jp in range(i):
                g_jp = gate[jp:jp + 1, :]
                wins = (g_jp > gate) | ((g_jp == gate) & (jp < blk_row))
                beaten = beaten + jnp.where(wins, 1.0, 0.0)
            chosen = (blk_row < i) & (beaten < MOBA_TOPK)
            penalty[u] = jnp.where(chosen, 0.0, NEG)

    def qk(u):
        i, hh = units[u]
        ts = []
        for j in range(i + 1):
            s = _dot_nt(kaug_ref[hh, j * ATT_BLOCK:(j + 1) * ATT_BLOCK, :], q_aug[u])
            if j == i:
                s = jnp.where(causal, s, NEG)
            elif i > MOBA_TOPK:
                s = s + penalty[u][j:j + 1, :]
            ts.append(s)
        m_el = ts[0]
        for s in ts[1:]:
            m_el = jnp.maximum(m_el, s)
        tiles[u] = ts
        col_max[u] = jnp.max(m_el, axis=0, keepdims=True)

    def expo(u):
        probs[u] = jnp.concatenate([jnp.exp(s - col_max[u]).astype(BF16) for s in tiles[u]], axis=0)

    def pv(u):
        i, hh = units[u]
        acc[u] = _dot(vaug_t_ref[hh, :, 0:(i + 1) * ATT_BLOCK], probs[u])

    def fin(u):
        i, hh = units[u]
        spare0 = (1 - hh) * HEAD_DIM
        outs[u] = acc[u] / acc[u][spare0:spare0 + 1, :]
        if hh == HEADS_PER_TILE - 1:
            o_pair_t = jnp.where(chan_t < HEAD_DIM, outs[u - 1], outs[u])
            o_ref[0, i * ATT_BLOCK:(i + 1) * ATT_BLOCK, :] = o_pair_t.T.astype(o_ref.dtype)

    stages = (prep, qk, expo, pv, fin)
    for t in range(n + (len(stages) - 1) * MOBA_SKEW):
        for k, stage in enumerate(stages):
            u = t - k * MOBA_SKEW
            if 0 <= u < n:
                stage(u)


def _moba_aux(seq):
    pos = jnp.arange(seq)
    lane = jnp.arange(LANES)
    kaux, qaux = [], []
    for hh in range(HEADS_PER_TILE):
        spare0 = (1 - hh) * HEAD_DIM
        j = lane[None, :] - spare0
        off = jnp.where(j == 0, (pos % MOBA_BLOCK)[:, None], 0).astype(F32)
        start = jnp.where(j == 1, ((pos // MOBA_BLOCK) * MOBA_BLOCK)[:, None], 0).astype(F32)
        kaux.append(off + start)
        qaux.append(((j == 0) | (j == 1))[0].astype(F32))
    kaux = jnp.stack(kaux).astype(BF16)
    qaux = jnp.stack(qaux)
    slopes = jnp.asarray([2.0 ** (-8.0 * (h + 1) / MOBA_HEADS) for h in range(MOBA_HEADS)], F32)
    slopes = slopes.reshape(MOBA_HEADS // HEADS_PER_TILE, HEADS_PER_TILE, 1)
    return kaux, qaux[None] * slopes


def _moba(proj3, q_col, k_col, v_col):
    b, seq, _ = proj3.shape
    nb = seq // MOBA_BLOCK
    assert seq % MOBA_BLOCK == 0 and nb <= SUBLANES
    pairs = MOBA_HEADS // HEADS_PER_TILE
    kaux, qaux = _moba_aux(seq)
    blk = lambda c0: pl.BlockSpec((1, seq, LANES), lambda bi, p: (bi, 0, c0 + p))
    return pl.pallas_call(
        _moba_kernel,
        out_shape=jax.ShapeDtypeStruct((b, seq, MIX_WIDTH), BF16),
        grid=(b, pairs),
        in_specs=[
            blk(q_col // LANES), blk(k_col // LANES), blk(v_col // LANES),
            _resident((HEADS_PER_TILE, seq, LANES)),
            pl.BlockSpec((1, HEADS_PER_TILE, LANES), lambda bi, p: (p, 0, 0)),
        ],
        out_specs=pl.BlockSpec((1, seq, LANES), lambda bi, p: (bi, 0, p)),
        scratch_shapes=[
            pltpu.VMEM((HEADS_PER_TILE, seq, LANES), BF16),
            pltpu.VMEM((HEADS_PER_TILE, LANES, seq), BF16),
            pltpu.VMEM((HEADS_PER_TILE, 2 * SUBLANES * (-(-nb // (2 * SUBLANES))), LANES), BF16),
        ],
        compiler_params=_params(2),
        name="moba",
    )(proj3, proj3, proj3, kaux, qaux)


SB_DEAD = 120.0
SB_SKEW = 2


def _sb_kernel(q_ref, k_ref, v_ref, tri_ref, o_ref, acc_ref, rest_ref):
    seq = q_ref.shape[1]
    nb = seq // ATT_BLOCK
    row_t = lax.broadcasted_iota(jnp.int32, (ATT_BLOCK, ATT_BLOCK), 0)
    col_t = lax.broadcasted_iota(jnp.int32, (ATT_BLOCK, ATT_BLOCK), 1)
    strict = col_t < row_t
    lane_q = lax.broadcasted_iota(jnp.int32, (ATT_BLOCK, LANES), 1)
    tri = tri_ref[...]
    key_rows = lambda j: slice(j * ATT_BLOCK, (j + 1) * ATT_BLOCK)

    def q_own(i, hh):
        q_i = q_ref[0, i * ATT_BLOCK:(i + 1) * ATT_BLOCK, :]
        own_q = (lane_q >= hh * HEAD_DIM) & (lane_q < (hh + 1) * HEAD_DIM)
        return jnp.where(own_q, q_i, jnp.zeros_like(q_i))

    def run_chains(chains, rests):
        n = len(chains)
        z, sp_bf, suffix, a = {}, {}, {}, {}
        outs, new_rests = [None] * n, [None] * n

        def scores(c):
            i, hh, j = chains[c]
            z[c] = _dot_nt(q_own(i, hh), k_ref[0, key_rows(j), :])

        def softplus(c):
            i, hh, j = chains[c]
            neg_abs = lax.bitcast_convert_type(
                lax.bitcast_convert_type(z[c], jnp.uint32) | jnp.uint32(0x80000000), F32)
            sp = jnp.maximum(z[c], 0.0) + jnp.log(1.0 + jnp.exp(neg_abs))
            if j == i:
                sp = jnp.where(strict, sp, 0.0)
            sp_bf[c] = sp.astype(BF16)

        def suffix_sum(c):
            suffix[c] = _dot(sp_bf[c], tri)

        def weights(c):
            i, hh, j = chains[c]
            w = jnp.exp(z[c] - suffix[c] - rests[c])
            if j == i:
                w = jnp.where(strict, w, 0.0)
            a[c] = w.astype(BF16)
            new_rests[c] = rests[c] + suffix[c][:, 0:1]

        def values(c):
            i, hh, j = chains[c]
            outs[c] = _dot(a[c], v_ref[0, key_rows(j), :])

        stages = (scores, softplus, suffix_sum, weights, values)
        for t in range(n + (len(stages) - 1) * SB_SKEW):
            for k, stage in enumerate(stages):
                c = t - k * SB_SKEW
                if 0 <= c < n:
                    stage(c)
        return outs, new_rests

    slot = lambda i, hh: i * HEADS_PER_TILE + hh
    units = lambda d: [(i, hh, i - d) for i in range(d, nb) for hh in range(HEADS_PER_TILE)]

    diag = units(0)
    acc, rest = run_chains(diag, [jnp.zeros((ATT_BLOCK, 1), F32)] * len(diag))
    acc = {(i, hh): o for (i, hh, _), o in zip(diag, acc)}
    rest = {(i, hh): r for (i, hh, _), r in zip(diag, rest)}
    if nb > 1:
        sub = units(1)
        outs, rests = run_chains(sub, [rest[(i, hh)] for (i, hh, _) in sub])
        for (i, hh, _), o, r in zip(sub, outs, rests):
            acc[(i, hh)] = acc[(i, hh)] + o
            rest[(i, hh)] = r
    for (i, hh), o in acc.items():
        acc_ref[slot(i, hh)] = o
        rest_ref[slot(i, hh)] = jnp.broadcast_to(rest[(i, hh)], (ATT_BLOCK, LANES))

    for d in range(2, nb):
        far = units(d)
        low = rest_ref[slot(d, 0)]
        for s in range(slot(d, 0) + 1, nb * HEADS_PER_TILE):
            low = jnp.minimum(low, rest_ref[s])

        @pl.when(jnp.min(low) < SB_DEAD)
        def _():
            outs, rests = run_chains(far, [rest_ref[slot(i, hh)][:, 0:1] for (i, hh, _) in far])
            for (i, hh, _), o, r in zip(far, outs, rests):
                acc_ref[slot(i, hh)] = acc_ref[slot(i, hh)] + o
                rest_ref[slot(i, hh)] = jnp.broadcast_to(r, (ATT_BLOCK, LANES))

    for i in range(nb):
        o_pair = jnp.where(lane_q < HEAD_DIM, acc_ref[slot(i, 0)], acc_ref[slot(i, 1)])
        o_ref[0, i * ATT_BLOCK:(i + 1) * ATT_BLOCK, :] = o_pair.astype(o_ref.dtype)


def _stick_breaking(proj3, q_col, k_col, v_col):
    b, seq, _ = proj3.shape
    assert seq % ATT_BLOCK == 0
    nb = seq // ATT_BLOCK
    pairs = SB_HEADS // HEADS_PER_TILE
    idx = jnp.arange(ATT_BLOCK)
    tri = (idx[:, None] >= idx[None, :]).astype(BF16)
    blk = lambda c0: pl.BlockSpec((1, seq, LANES), lambda bi, p: (bi, 0, c0 + p))
    state = pltpu.VMEM((nb * HEADS_PER_TILE, ATT_BLOCK, LANES), F32)
    return pl.pallas_call(
        _sb_kernel,
        out_shape=jax.ShapeDtypeStruct((b, seq, MIX_WIDTH), BF16),
        grid=(b, pairs),
        in_specs=[
            blk(q_col // LANES), blk(k_col // LANES), blk(v_col // LANES),
            _resident(tri.shape),
        ],
        out_specs=pl.BlockSpec((1, seq, LANES), lambda bi, p: (bi, 0, p)),
        scratch_shapes=[state, state],
        compiler_params=_params(2),
        name="stick_breaking",
    )(proj3, proj3, proj3, tri)


def _mem_kv_kernel(mem_ref, g_ref, w_ref, o_ref):
    h = _rms_scale(mem_ref[0], g_ref[...]).astype(BF16)
    o_ref[0] = _dot(h, w_ref[...]).astype(o_ref.dtype)


def _mem_kv(mem, g, w):
    b, m, d = mem.shape
    return pl.pallas_call(
        _mem_kv_kernel,
        out_shape=jax.ShapeDtypeStruct((b, m, w.shape[1]), BF16),
        grid=(b,),
        in_specs=[pl.BlockSpec((1, m, d), lambda i: (i, 0, 0)), _resident((1, d)), _resident(w.shape)],
        out_specs=pl.BlockSpec((1, m, w.shape[1]), lambda i: (i, 0, 0)),
        compiler_params=_params(1),
        name="mem_kv",
    )(mem, g, w)


def _mix_cross_kernel(a_ref, b_ref, ga_ref, gb_ref, x_ref, kv_ref, g_ref,
                      wa_ref, wb_ref, wo_ref, wq_ref, wxo_ref, o_ref):
    half = x_ref.shape[1] // 2
    rows = (slice(0, half), slice(half, 2 * half))
    ya, yb, merged, x1, q, probs, denom, attn = ({} for _ in range(8))
    head_cols = lambda hd: slice(hd * X_HEAD_DIM, (hd + 1) * X_HEAD_DIM)

    def branches(r):
        ya[r] = _dot(a_ref[0, rows[r], :], wa_ref[...])
        yb[r] = _dot(b_ref[0, rows[r], :], wb_ref[...])

    def merge(r):
        m = ga_ref[0, rows[r], :].astype(F32) * ya[r] + gb_ref[0, rows[r], :].astype(F32) * yb[r]
        merged[r] = m.astype(BF16)

    def mix_residual(r):
        x1[r] = x_ref[0, rows[r], :] + _dot(merged[r], wo_ref[...])

    def project(r):
        h = _rms_scale(x1[r], g_ref[...]).astype(BF16)
        q[r] = (_dot(h, wq_ref[...]) * (X_HEAD_DIM ** -0.5)).astype(BF16)

    def scores(r):
        probs[r], denom[r] = [], []
        for hd in range(X_HEADS):
            s = _dot_nt(q[r][:, head_cols(hd)], kv_ref[0, :, head_cols(hd)])
            p = jnp.exp(s - jnp.max(s, axis=1, keepdims=True))
            denom[r].append(jnp.sum(p, axis=1, keepdims=True))
            probs[r].append(p.astype(BF16))

    def values(r):
        heads = []
        for hd in range(X_HEADS):
            v_h = kv_ref[0, :, X_WIDTH + hd * X_HEAD_DIM:X_WIDTH + (hd + 1) * X_HEAD_DIM]
            heads.append((_dot(probs[r][hd], v_h) / denom[r][hd]).astype(BF16))
        attn[r] = jnp.concatenate(heads, axis=1)

    def cross_residual(r):
        o_ref[0, rows[r], :] = x1[r] + _dot(attn[r], wxo_ref[...])

    stages = (branches, merge, mix_residual, project, scores, values, cross_residual)
    for t in range(len(rows) + len(stages) - 1):
        for k, stage in enumerate(stages):
            if 0 <= t - k < len(rows):
                stage(t - k)


def _mix_cross(moba_o, sb_o, proj3, gate_col, x3, kv, g, wa, wb, wo, wq, wxo):
    b, seq, d = x3.shape
    m = kv.shape[1]
    rows = lambda w, c=0: pl.BlockSpec((1, ROW_TILE, w), lambda bi, i: (bi, i, c))
    return pl.pallas_call(
        _mix_cross_kernel,
        out_shape=jax.ShapeDtypeStruct((b, seq, d), F32),
        grid=(b, seq // ROW_TILE),
        in_specs=[
            rows(MIX_WIDTH), rows(MIX_WIDTH),
            rows(d, gate_col // d), rows(d, gate_col // d + 1),
            rows(d),
            pl.BlockSpec((1, m, kv.shape[2]), lambda bi, i: (bi, 0, 0)),
            _resident((1, d)),
            _resident(wa.shape), _resident(wb.shape), _resident(wo.shape),
            _resident(wq.shape), _resident(wxo.shape),
        ],
        out_specs=rows(d),
        compiler_params=_params(2),
        name="mix_cross",
    )(moba_o, sb_o, proj3, proj3, x3, kv, g, wa, wb, wo, wq, wxo)


def _mlp_kernel(x_ref, g_ref, wu_ref, wd_ref, gf_ref, o_ref, *, final_norm):
    x = x_ref[...]
    h = _rms_scale(x, g_ref[...]).astype(BF16)
    acc = x
    for c in range(wu_ref.shape[1] // COL_CHUNK):
        cols = slice(c * COL_CHUNK, (c + 1) * COL_CHUNK)
        u = jnp.maximum(_dot(h, wu_ref[:, cols]), 0.0)
        acc = acc + _dot((u * u).astype(BF16), wd_ref[cols, :])
    o_ref[...] = _rms_scale(acc, gf_ref[...]) if final_norm else acc


def _mlp(x2d, g, wu, wd, gf, final_norm):
    n, d = x2d.shape
    return pl.pallas_call(
        functools.partial(_mlp_kernel, final_norm=final_norm),
        out_shape=jax.ShapeDtypeStruct((n, d), F32),
        grid=(n // ROW_TILE,),
        in_specs=[
            pl.BlockSpec((ROW_TILE, d), lambda i: (i, 0)),
            _resident((1, d)), _resident(wu.shape), _resident(wd.shape), _resident((1, d)),
        ],
        out_specs=pl.BlockSpec((ROW_TILE, d), lambda i: (i, 0)),
        compiler_params=_params(1),
        name="mlp_final",
    )(x2d, g, wu, wd, gf)


def kernel(x, mem, g_mix, w_in, w_br_moba, w_br_sb, w_out, g_cross, g_mem,
           w_xq, w_xkv, w_xo, g_mlp, w_up, w_down, g_final):
    b, seq, d = x.shape
    n = b * seq
    assert n % ROW_TILE == 0 and seq % ROW_TILE == 0
    row = lambda v: v.reshape(1, -1).astype(F32)
    depth = g_mix.shape[0]
    x2d = x.reshape(n, d)
    for l in range(depth):
        proj = _in_proj(x2d, row(g_mix[l]), w_in[l].astype(BF16))
        proj3 = proj.reshape(b, seq, -1)
        moba_o = _moba(proj3, 0, MIX_WIDTH, 2 * MIX_WIDTH)
        sb_o = _stick_breaking(proj3, 3 * MIX_WIDTH, 4 * MIX_WIDTH, 5 * MIX_WIDTH)
        kv = _mem_kv(mem, row(g_mem[l]), w_xkv[l].astype(BF16))
        x3 = _mix_cross(moba_o, sb_o, proj3, 6 * MIX_WIDTH, x2d.reshape(b, seq, d), kv, row(g_cross[l]),
                        w_br_moba[l].astype(BF16), w_br_sb[l].astype(BF16), w_out[l].astype(BF16),
                        w_xq[l].astype(BF16), w_xo[l].astype(BF16))
        x2d = _mlp(x3.reshape(n, d), row(g_mlp[l]), w_up[l].astype(BF16), w_down[l].astype(BF16),
                   row(g_final), final_norm=(l == depth - 1))
    return x2d.reshape(b, seq, d)
```

```python
import functools

import jax
import jax.numpy as jnp
from jax import lax
from jax.experimental import pallas as pl
from jax.experimental.pallas import tpu as pltpu

F32 = jnp.float32
BF16 = jnp.bfloat16

HEAD_DIM = 64
MOBA_HEADS = 8
SB_HEADS = 8
MIX_WIDTH = MOBA_HEADS * HEAD_DIM
MOBA_BLOCK = 256
MOBA_TOPK = 3
X_HEADS = 4
X_HEAD_DIM = 128
X_WIDTH = X_HEADS * X_HEAD_DIM
RMS_EPS = 1e-6
NEG = -1e30

LANES = 128
SUBLANES = 8
HEADS_PER_TILE = LANES // HEAD_DIM
ATT_BLOCK = 256
ROW_TILE = 512
COL_CHUNK = 512
VMEM_LIMIT = 56 * 1024 * 1024


def _params(n_axes):
    return pltpu.CompilerParams(
        dimension_semantics=("arbitrary",) * n_axes, vmem_limit_bytes=VMEM_LIMIT)


def _resident(shape):
    return pl.BlockSpec(shape, lambda *_: (0,) * len(shape), pipeline_mode=pl.Buffered(1))


def _rms_scale(x, g):
    y = x * lax.rsqrt(jnp.mean(x * x, axis=-1, keepdims=True) + RMS_EPS)
    return y * g


def _dot(a, b):
    return jnp.dot(a, b, preferred_element_type=F32)


def _dot_nt(a, b):
    return lax.dot_general(a, b, (((1,), (1,)), ((), ())), preferred_element_type=F32)


def _in_proj_kernel(x_ref, g_ref, w_ref, o_ref, *, q_chunks, gate_start):
    n_chunks = w_ref.shape[1] // COL_CHUNK
    half = x_ref.shape[0] // 2
    rows = (slice(0, half), slice(half, 2 * half))
    h = {}

    def norm(r):
        h[r] = _rms_scale(x_ref[rows[r], :], g_ref[...]).astype(BF16)

    def chunk(r, c):
        cols = slice(c * COL_CHUNK, (c + 1) * COL_CHUNK)
        acc = _dot(h[r], w_ref[:, cols])
        if c in q_chunks:
            acc = acc * (HEAD_DIM ** -0.5)
        if c >= gate_start:
            acc = 1.0 / (1.0 + jnp.exp(-acc))
        o_ref[rows[r], cols] = acc.astype(o_ref.dtype)

    norm(0)
    chunk(0, 0)
    norm(1)
    for c in range(1, n_chunks):
        chunk(1, c - 1)
        chunk(0, c)
    chunk(1, n_chunks - 1)


def _in_proj(x2d, g, w):
    n, d = x2d.shape
    cols = w.shape[1]
    q_chunks = (0, 3 * MIX_WIDTH // COL_CHUNK)
    gate_start = 6 * MIX_WIDTH // COL_CHUNK
    return pl.pallas_call(
        functools.partial(_in_proj_kernel, q_chunks=q_chunks, gate_start=gate_start),
        out_shape=jax.ShapeDtypeStruct((n, cols), BF16),
        grid=(n // ROW_TILE,),
        in_specs=[
            pl.BlockSpec((ROW_TILE, d), lambda i: (i, 0)),
            _resident((1, d)),
            _resident((d, cols)),
        ],
        out_specs=pl.BlockSpec((ROW_TILE, cols), lambda i: (i, 0)),
        compiler_params=_params(1),
        name="in_proj",
    )(x2d, g, w)


MOBA_SKEW = 1


def _moba_kernel(q_ref, k_ref, v_ref, kaux_ref, qaux_ref, o_ref,
                 kaug_ref, vaug_t_ref, km_ref):
    seq = q_ref.shape[1]
    nb = seq // ATT_BLOCK
    lane = lax.broadcasted_iota(jnp.int32, (1, LANES), 1)
    key_t = lax.broadcasted_iota(jnp.int32, (ATT_BLOCK, ATT_BLOCK), 0)
    qry_t = lax.broadcasted_iota(jnp.int32, (ATT_BLOCK, ATT_BLOCK), 1)
    causal = key_t <= qry_t
    lane_q = lax.broadcasted_iota(jnp.int32, (ATT_BLOCK, LANES), 1)
    chan_t = lax.broadcasted_iota(jnp.int32, (LANES, ATT_BLOCK), 0)
    nb_rows = km_ref.shape[1]
    blk_row = lax.broadcasted_iota(jnp.int32, (SUBLANES, ATT_BLOCK), 0)

    k_all = k_ref[0]
    v_all = v_ref[0]
    means = [jnp.mean(k_all[j * ATT_BLOCK:(j + 1) * ATT_BLOCK].astype(F32), axis=0, keepdims=True)
             for j in range(nb)]
    means.append(jnp.zeros((nb_rows - nb, LANES), F32))
    means = jnp.concatenate(means, axis=0)
    v_t = v_all.astype(F32).T
    chan_all = lax.broadcasted_iota(jnp.int32, (LANES, seq), 0)
    for hh in range(HEADS_PER_TILE):
        own = (lane >= hh * HEAD_DIM) & (lane < (hh + 1) * HEAD_DIM)
        own_t = (chan_all >= hh * HEAD_DIM) & (chan_all < (hh + 1) * HEAD_DIM)
        kaug_ref[hh] = jnp.where(own, k_all, kaux_ref[hh])
        vaug_t_ref[hh] = jnp.where(own_t, v_t, 1.0).astype(BF16)
        km_ref[hh] = jnp.where(own, means, 0.0).astype(BF16)

    units = [(i, hh) for i in range(nb - 1, -1, -1) for hh in range(HEADS_PER_TILE)]
    n = len(units)
    q_aug, penalty, tiles, col_max, probs, acc, outs = ({} for _ in range(7))

    def prep(u):
        i, hh = units[u]
        q_i = q_ref[0, i * ATT_BLOCK:(i + 1) * ATT_BLOCK, :]
        own_q = (lane_q >= hh * HEAD_DIM) & (lane_q < (hh + 1) * HEAD_DIM)
        q_aug[u] = jnp.where(own_q, q_i, qaux_ref[0, hh:hh + 1, :].astype(BF16))
        if i > MOBA_TOPK:
            gate = _dot_nt(km_ref[hh], q_i)[:SUBLANES]
            beaten = jnp.zeros((SUBLANES, ATT_BLOCK), F32)
            for jp in range(i):
                g_jp = gate[jp:jp + 1, :]
                wins = (g_jp > gate) | ((g_jp == gate) & (jp < blk_row))
                beaten = beaten + jnp.where(wins, 1.0, 0.0)
            chosen = (blk_row < i) & (beaten < MOBA_TOPK)
            penalty[u] = jnp.where(chosen, 0.0, NEG)

    def qk(u):
        i, hh = units[u]
        ts = []
        for j in range(i + 1):
            s = _dot_nt(kaug_ref[hh, j * ATT_BLOCK:(j + 1) * ATT_BLOCK, :], q_aug[u])
            if j == i:
                s = jnp.where(causal, s, NEG)
            elif i > MOBA_TOPK:
                s = s + penalty[u][j:j + 1, :]
            ts.append(s)
        m_el = ts[0]
        for s in ts[1:]:
            m_el = jnp.maximum(m_el, s)
        tiles[u] = ts
        col_max[u] = jnp.max(m_el, axis=0, keepdims=True)

    def expo(u):
        probs[u] = jnp.concatenate([jnp.exp(s - col_max[u]).astype(BF16) for s in tiles[u]], axis=0)

    def pv(u):
        i, hh = units[u]
        acc[u] = _dot(vaug_t_ref[hh, :, 0:(i + 1) * ATT_BLOCK], probs[u])

    def fin(u):
        i, hh = units[u]
        spare0 = (1 - hh) * HEAD_DIM
        outs[u] = acc[u] / acc[u][spare0:spare0 + 1, :]
        if hh == HEADS_PER_TILE - 1:
            o_pair_t = jnp.where(chan_t < HEAD_DIM, outs[u - 1], outs[u])
            o_ref[0, i * ATT_BLOCK:(i + 1) * ATT_BLOCK, :] = o_pair_t.T.astype(o_ref.dtype)

    stages = (prep, qk, expo, pv, fin)
    for t in range(n + (len(stages) - 1) * MOBA_SKEW):
        for k, stage in enumerate(stages):
            u = t - k * MOBA_SKEW
            if 0 <= u < n:
                stage(u)


def _moba_aux(seq):
    pos = jnp.arange(seq)
    lane = jnp.arange(LANES)
    kaux, qaux = [], []
    for hh in range(HEADS_PER_TILE):
        spare0 = (1 - hh) * HEAD_DIM
        j = lane[None, :] - spare0
        off = jnp.where(j == 0, (pos % MOBA_BLOCK)[:, None], 0).astype(F32)
        start = jnp.where(j == 1, ((pos // MOBA_BLOCK) * MOBA_BLOCK)[:, None], 0).astype(F32)
        kaux.append(off + start)
        qaux.append(((j == 0) | (j == 1))[0].astype(F32))
    kaux = jnp.stack(kaux).astype(BF16)
    qaux = jnp.stack(qaux)
    slopes = jnp.asarray([2.0 ** (-8.0 * (h + 1) / MOBA_HEADS) for h in range(MOBA_HEADS)], F32)
    slopes = slopes.reshape(MOBA_HEADS // HEADS_PER_TILE, HEADS_PER_TILE, 1)
    return kaux, qaux[None] * slopes


def _moba(proj3, q_col, k_col, v_col):
    b, seq, _ = proj3.shape
    nb = seq // MOBA_BLOCK
    assert seq % MOBA_BLOCK == 0 and nb <= SUBLANES
    pairs = MOBA_HEADS // HEADS_PER_TILE
    kaux, qaux = _moba_aux(seq)
    blk = lambda c0: pl.BlockSpec((1, seq, LANES), lambda bi, p: (bi, 0, c0 + p))
    return pl.pallas_call(
        _moba_kernel,
        out_shape=jax.ShapeDtypeStruct((b, seq, MIX_WIDTH), BF16),
        grid=(b, pairs),
        in_specs=[
            blk(q_col // LANES), blk(k_col // LANES), blk(v_col // LANES),
            _resident((HEADS_PER_TILE, seq, LANES)),
            pl.BlockSpec((1, HEADS_PER_TILE, LANES), lambda bi, p: (p, 0, 0)),
        ],
        out_specs=pl.BlockSpec((1, seq, LANES), lambda bi, p: (bi, 0, p)),
        scratch_shapes=[
            pltpu.VMEM((HEADS_PER_TILE, seq, LANES), BF16),
            pltpu.VMEM((HEADS_PER_TILE, LANES, seq), BF16),
            pltpu.VMEM((HEADS_PER_TILE, 2 * SUBLANES * (-(-nb // (2 * SUBLANES))), LANES), BF16),
        ],
        compiler_params=_params(2),
        name="moba",
    )(proj3, proj3, proj3, kaux, qaux)


SB_DEAD = 120.0
SB_SKEW = 2


def _sb_kernel(q_ref, k_ref, v_ref, tri_ref, o_ref, acc_ref, rest_ref):
    seq = q_ref.shape[1]
    nb = seq // ATT_BLOCK
    row_t = lax.broadcasted_iota(jnp.int32, (ATT_BLOCK, ATT_BLOCK), 0)
    col_t = lax.broadcasted_iota(jnp.int32, (ATT_BLOCK, ATT_BLOCK), 1)
    strict = col_t < row_t
    lane_q = lax.broadcasted_iota(jnp.int32, (ATT_BLOCK, LANES), 1)
    tri = tri_ref[...]
    key_rows = lambda j: slice(j * ATT_BLOCK, (j + 1) * ATT_BLOCK)

    def q_own(i, hh):
        q_i = q_ref[0, i * ATT_BLOCK:(i + 1) * ATT_BLOCK, :]
        own_q = (lane_q >= hh * HEAD_DIM) & (lane_q < (hh + 1) * HEAD_DIM)
        return jnp.where(own_q, q_i, jnp.zeros_like(q_i))

    def run_chains(chains, rest_of):
        n = len(chains)
        z, sp_bf, suffix, a = {}, {}, {}, {}
        outs, new_rests = [None] * n, [None] * n

        def scores(c):
            i, hh, j = chains[c]
            z[c] = _dot_nt(q_own(i, hh), k_ref[0, key_rows(j), :])

        def softplus(c):
            i, hh, j = chains[c]
            neg_abs = lax.bitcast_convert_type(
                lax.bitcast_convert_type(z[c], jnp.uint32) | jnp.uint32(0x80000000), F32)
            sp = jnp.maximum(z[c], 0.0) + jnp.log(1.0 + jnp.exp(neg_abs))
            if j == i:
                sp = jnp.where(strict, sp, 0.0)
            sp_bf[c] = sp.astype(BF16)

        def suffix_sum(c):
            suffix[c] = _dot(sp_bf[c], tri)

        def weights(c):
            i, hh, j = chains[c]
            rest = rest_of(c, new_rests)
            w = jnp.exp(z[c] - suffix[c] - rest)
            if j == i:
                w = jnp.where(strict, w, 0.0)
            a[c] = w.astype(BF16)
            new_rests[c] = rest + suffix[c][:, 0:1]

        def values(c):
            i, hh, j = chains[c]
            outs[c] = _dot(a[c], v_ref[0, key_rows(j), :])

        stages = (scores, softplus, suffix_sum, weights, values)
        for t in range(n + (len(stages) - 1) * SB_SKEW):
            for k, stage in enumerate(stages):
                c = t - k * SB_SKEW
                if 0 <= c < n:
                    stage(c)
        return outs, new_rests

    slot = lambda i, hh: i * HEADS_PER_TILE + hh
    units = lambda d: [(i, hh, i - d) for i in range(d, nb) for hh in range(HEADS_PER_TILE)]

    near = units(0) + (units(1) if nb > 1 else [])
    where = {unit: c for c, unit in enumerate(near)}
    zero_rest = jnp.zeros((ATT_BLOCK, 1), F32)

    def near_rest(c, done):
        i, hh, j = near[c]
        return zero_rest if j == i else done[where[(i, hh, i)]]

    outs, rests = run_chains(near, near_rest)
    for (i, hh, j), o, r in zip(near, outs, rests):
        if j == i and (i, hh, i - 1) in where:
            continue
        total = o if j == i else o + outs[where[(i, hh, i)]]
        acc_ref[slot(i, hh)] = total
        rest_ref[slot(i, hh)] = jnp.broadcast_to(r, (ATT_BLOCK, LANES))

    for d in range(2, nb):
        far = units(d)
        low = rest_ref[slot(d, 0)]
        for s in range(slot(d, 0) + 1, nb * HEADS_PER_TILE):
            low = jnp.minimum(low, rest_ref[s])

        @pl.when(jnp.min(low) < SB_DEAD)
        def _():
            outs, rests = run_chains(far, lambda c, _: rest_ref[slot(far[c][0], far[c][1])][:, 0:1])
            for (i, hh, _), o, r in zip(far, outs, rests):
                acc_ref[slot(i, hh)] = acc_ref[slot(i, hh)] + o
                rest_ref[slot(i, hh)] = jnp.broadcast_to(r, (ATT_BLOCK, LANES))

    for i in range(nb):
        o_pair = jnp.where(lane_q < HEAD_DIM, acc_ref[slot(i, 0)], acc_ref[slot(i, 1)])
        o_ref[0, i * ATT_BLOCK:(i + 1) * ATT_BLOCK, :] = o_pair.astype(o_ref.dtype)


def _stick_breaking(proj3, q_col, k_col, v_col):
    b, seq, _ = proj3.shape
    assert seq % ATT_BLOCK == 0
    nb = seq // ATT_BLOCK
    pairs = SB_HEADS // HEADS_PER_TILE
    idx = jnp.arange(ATT_BLOCK)
    tri = (idx[:, None] >= idx[None, :]).astype(BF16)
    blk = lambda c0: pl.BlockSpec((1, seq, LANES), lambda bi, p: (bi, 0, c0 + p))
    state = pltpu.VMEM((nb * HEADS_PER_TILE, ATT_BLOCK, LANES), F32)
    return pl.pallas_call(
        _sb_kernel,
        out_shape=jax.ShapeDtypeStruct((b, seq, MIX_WIDTH), BF16),
        grid=(b, pairs),
        in_specs=[
            blk(q_col // LANES), blk(k_col // LANES), blk(v_col // LANES),
            _resident(tri.shape),
        ],
        out_specs=pl.BlockSpec((1, seq, LANES), lambda bi, p: (bi, 0, p)),
        scratch_shapes=[state, state],
        compiler_params=_params(2),
        name="stick_breaking",
    )(proj3, proj3, proj3, tri)


def _mem_kv_kernel(mem_ref, g_ref, w_ref, o_ref):
    h = _rms_scale(mem_ref[0], g_ref[...]).astype(BF16)
    o_ref[0] = _dot(h, w_ref[...]).astype(o_ref.dtype)


def _mem_kv(mem, g, w):
    b, m, d = mem.shape
    return pl.pallas_call(
        _mem_kv_kernel,
        out_shape=jax.ShapeDtypeStruct((b, m, w.shape[1]), BF16),
        grid=(b,),
        in_specs=[pl.BlockSpec((1, m, d), lambda i: (i, 0, 0)), _resident((1, d)), _resident(w.shape)],
        out_specs=pl.BlockSpec((1, m, w.shape[1]), lambda i: (i, 0, 0)),
        compiler_params=_params(1),
        name="mem_kv",
    )(mem, g, w)


def _mix_cross_kernel(a_ref, b_ref, ga_ref, gb_ref, x_ref, kv_ref, g_ref,
                      wa_ref, wb_ref, wo_ref, wq_ref, wxo_ref, o_ref):
    half = x_ref.shape[1] // 2
    rows = (slice(0, half), slice(half, 2 * half))
    ya, yb, merged, x1, q, probs, denom, attn = ({} for _ in range(8))
    head_cols = lambda hd: slice(hd * X_HEAD_DIM, (hd + 1) * X_HEAD_DIM)

    def branches(r):
        ya[r] = _dot(a_ref[0, rows[r], :], wa_ref[...])
        yb[r] = _dot(b_ref[0, rows[r], :], wb_ref[...])

    def merge(r):
        m = ga_ref[0, rows[r], :].astype(F32) * ya[r] + gb_ref[0, rows[r], :].astype(F32) * yb[r]
        merged[r] = m.astype(BF16)

    def mix_residual(r):
        x1[r] = x_ref[0, rows[r], :] + _dot(merged[r], wo_ref[...])

    def project(r):
        h = _rms_scale(x1[r], g_ref[...]).astype(BF16)
        q[r] = (_dot(h, wq_ref[...]) * (X_HEAD_DIM ** -0.5)).astype(BF16)

    def scores(r):
        probs[r], denom[r] = [], []
        for hd in range(X_HEADS):
            s = _dot_nt(q[r][:, head_cols(hd)], kv_ref[0, :, head_cols(hd)])
            p = jnp.exp(s - jnp.max(s, axis=1, keepdims=True))
            denom[r].append(jnp.sum(p, axis=1, keepdims=True))
            probs[r].append(p.astype(BF16))

    def values(r):
        heads = []
        for hd in range(X_HEADS):
            v_h = kv_ref[0, :, X_WIDTH + hd * X_HEAD_DIM:X_WIDTH + (hd + 1) * X_HEAD_DIM]
            heads.append((_dot(probs[r][hd], v_h) / denom[r][hd]).astype(BF16))
        attn[r] = jnp.concatenate(heads, axis=1)

    def cross_residual(r):
        o_ref[0, rows[r], :] = x1[r] + _dot(attn[r], wxo_ref[...])

    stages = (branches, merge, mix_residual, project, scores, values, cross_residual)
    for t in range(len(rows) + len(stages) - 1):
        for k, stage in enumerate(stages):
            if 0 <= t - k < len(rows):
                stage(t - k)


def _mix_cross(moba_o, sb_o, proj3, gate_col, x3, kv, g, wa, wb, wo, wq, wxo):
    b, seq, d = x3.shape
    m = kv.shape[1]
    rows = lambda w, c=0: pl.BlockSpec((1, ROW_TILE, w), lambda bi, i: (bi, i, c))
    return pl.pallas_call(
        _mix_cross_kernel,
        out_shape=jax.ShapeDtypeStruct((b, seq, d), F32),
        grid=(b, seq // ROW_TILE),
        in_specs=[
            rows(MIX_WIDTH), rows(MIX_WIDTH),
            rows(d, gate_col // d), rows(d, gate_col // d + 1),
            rows(d),
            pl.BlockSpec((1, m, kv.shape[2]), lambda bi, i: (bi, 0, 0)),
            _resident((1, d)),
            _resident(wa.shape), _resident(wb.shape), _resident(wo.shape),
            _resident(wq.shape), _resident(wxo.shape),
        ],
        out_specs=rows(d),
        compiler_params=_params(2),
        name="mix_cross",
    )(moba_o, sb_o, proj3, proj3, x3, kv, g, wa, wb, wo, wq, wxo)


def _mlp_kernel(x_ref, g_ref, wu_ref, wd_ref, gf_ref, o_ref, *, final_norm):
    x = x_ref[...]
    h = _rms_scale(x, g_ref[...]).astype(BF16)
    acc = x
    for c in range(wu_ref.shape[1] // COL_CHUNK):
        cols = slice(c * COL_CHUNK, (c + 1) * COL_CHUNK)
        u = jnp.maximum(_dot(h, wu_ref[:, cols]), 0.0)
        acc = acc + _dot((u * u).astype(BF16), wd_ref[cols, :])
    o_ref[...] = _rms_scale(acc, gf_ref[...]) if final_norm else acc


def _mlp(x2d, g, wu, wd, gf, final_norm):
    n, d = x2d.shape
    return pl.pallas_call(
        functools.partial(_mlp_kernel, final_norm=final_norm),
        out_shape=jax.ShapeDtypeStruct((n, d), F32),
        grid=(n // ROW_TILE,),
        in_specs=[
            pl.BlockSpec((ROW_TILE, d), lambda i: (i, 0)),
            _resident((1, d)), _resident(wu.shape), _resident(wd.shape), _resident((1, d)),
        ],
        out_specs=pl.BlockSpec((ROW_TILE, d), lambda i: (i, 0)),
        compiler_params=_params(1),
        name="mlp_final",
    )(x2d, g, wu, wd, gf)


def kernel(x, mem, g_mix, w_in, w_br_moba, w_br_sb, w_out, g_cross, g_mem,
           w_xq, w_xkv, w_xo, g_mlp, w_up, w_down, g_final):
    b, seq, d = x.shape
    n = b * seq
    assert n % ROW_TILE == 0 and seq % ROW_TILE == 0
    row = lambda v: v.reshape(1, -1).astype(F32)
    depth = g_mix.shape[0]
    x2d = x.reshape(n, d)
    for l in range(depth):
        proj = _in_proj(x2d, row(g_mix[l]), w_in[l].astype(BF16))
        proj3 = proj.reshape(b, seq, -1)
        moba_o = _moba(proj3, 0, MIX_WIDTH, 2 * MIX_WIDTH)
        sb_o = _stick_breaking(proj3, 3 * MIX_WIDTH, 4 * MIX_WIDTH, 5 * MIX_WIDTH)
        kv = _mem_kv(mem, row(g_mem[l]), w_xkv[l].astype(BF16))
        x3 = _mix_cross(moba_o, sb_o, proj3, 6 * MIX_WIDTH, x2d.reshape(b, seq, d), kv, row(g_cross[l]),
                        w_br_moba[l].astype(BF16), w_br_sb[l].astype(BF16), w_out[l].astype(BF16),
                        w_xq[l].astype(BF16), w_xo[l].astype(BF16))
        x2d = _mlp(x3.reshape(n, d), row(g_mlp[l]), w_up[l].astype(BF16), w_down[l].astype(BF16),
                   row(g_final), final_norm=(l == depth - 1))
    return x2d.reshape(b, seq, d)
```

```python
import functools

import jax
import jax.numpy as jnp
from jax import lax
from jax.experimental import pallas as pl
from jax.experimental.pallas import tpu as pltpu

F32 = jnp.float32
BF16 = jnp.bfloat16

HEAD_DIM = 64
MOBA_HEADS = 8
SB_HEADS = 8
MIX_WIDTH = MOBA_HEADS * HEAD_DIM
MOBA_BLOCK = 256
MOBA_TOPK = 3
X_HEADS = 4
X_HEAD_DIM = 128
X_WIDTH = X_HEADS * X_HEAD_DIM
RMS_EPS = 1e-6
NEG = -1e30

LANES = 128
SUBLANES = 8
HEADS_PER_TILE = LANES // HEAD_DIM
ATT_BLOCK = 256
ROW_TILE = 512
COL_CHUNK = 512
VMEM_LIMIT = 56 * 1024 * 1024


def _params(n_axes):
    return pltpu.CompilerParams(
        dimension_semantics=("arbitrary",) * n_axes, vmem_limit_bytes=VMEM_LIMIT)


def _resident(shape):
    return pl.BlockSpec(shape, lambda *_: (0,) * len(shape), pipeline_mode=pl.Buffered(1))


def _rms_scale(x, g):
    y = x * lax.rsqrt(jnp.mean(x * x, axis=-1, keepdims=True) + RMS_EPS)
    return y * g


def _dot(a, b):
    return jnp.dot(a, b, preferred_element_type=F32)


def _dot_nt(a, b):
    return lax.dot_general(a, b, (((1,), (1,)), ((), ())), preferred_element_type=F32)


def _in_proj_kernel(x_ref, g_ref, w_ref, o_ref, *, q_chunks, gate_start):
    n_chunks = w_ref.shape[1] // COL_CHUNK
    half = x_ref.shape[0] // 2
    rows = (slice(0, half), slice(half, 2 * half))
    h = {}

    def norm(r):
        h[r] = _rms_scale(x_ref[rows[r], :], g_ref[...]).astype(BF16)

    def chunk(r, c):
        cols = slice(c * COL_CHUNK, (c + 1) * COL_CHUNK)
        acc = _dot(h[r], w_ref[:, cols])
        if c in q_chunks:
            acc = acc * (HEAD_DIM ** -0.5)
        if c >= gate_start:
            acc = 1.0 / (1.0 + jnp.exp(-acc))
        o_ref[rows[r], cols] = acc.astype(o_ref.dtype)

    norm(0)
    chunk(0, 0)
    norm(1)
    for c in range(1, n_chunks):
        chunk(1, c - 1)
        chunk(0, c)
    chunk(1, n_chunks - 1)


def _in_proj(x2d, g, w):
    n, d = x2d.shape
    cols = w.shape[1]
    q_chunks = (0, 3 * MIX_WIDTH // COL_CHUNK)
    gate_start = 6 * MIX_WIDTH // COL_CHUNK
    return pl.pallas_call(
        functools.partial(_in_proj_kernel, q_chunks=q_chunks, gate_start=gate_start),
        out_shape=jax.ShapeDtypeStruct((n, cols), BF16),
        grid=(n // ROW_TILE,),
        in_specs=[
            pl.BlockSpec((ROW_TILE, d), lambda i: (i, 0)),
            _resident((1, d)),
            _resident((d, cols)),
        ],
        out_specs=pl.BlockSpec((ROW_TILE, cols), lambda i: (i, 0)),
        compiler_params=_params(1),
        name="in_proj",
    )(x2d, g, w)


MOBA_SKEW = 1


def _moba_kernel(q_ref, k_ref, v_ref, kaux_ref, qaux_ref, o_ref,
                 kaug_ref, vaug_t_ref, km_ref):
    seq = q_ref.shape[1]
    nb = seq // ATT_BLOCK
    lane = lax.broadcasted_iota(jnp.int32, (1, LANES), 1)
    key_t = lax.broadcasted_iota(jnp.int32, (ATT_BLOCK, ATT_BLOCK), 0)
    qry_t = lax.broadcasted_iota(jnp.int32, (ATT_BLOCK, ATT_BLOCK), 1)
    causal = key_t <= qry_t
    lane_q = lax.broadcasted_iota(jnp.int32, (ATT_BLOCK, LANES), 1)
    chan_t = lax.broadcasted_iota(jnp.int32, (LANES, ATT_BLOCK), 0)
    nb_rows = km_ref.shape[1]
    blk_row = lax.broadcasted_iota(jnp.int32, (SUBLANES, ATT_BLOCK), 0)

    k_all = k_ref[0]
    v_all = v_ref[0]
    means = [jnp.mean(k_all[j * ATT_BLOCK:(j + 1) * ATT_BLOCK].astype(F32), axis=0, keepdims=True)
             for j in range(nb)]
    means.append(jnp.zeros((nb_rows - nb, LANES), F32))
    means = jnp.concatenate(means, axis=0)
    v_t = v_all.astype(F32).T
    chan_all = lax.broadcasted_iota(jnp.int32, (LANES, seq), 0)
    for hh in range(HEADS_PER_TILE):
        own = (lane >= hh * HEAD_DIM) & (lane < (hh + 1) * HEAD_DIM)
        own_t = (chan_all >= hh * HEAD_DIM) & (chan_all < (hh + 1) * HEAD_DIM)
        kaug_ref[hh] = jnp.where(own, k_all, kaux_ref[hh])
        vaug_t_ref[hh] = jnp.where(own_t, v_t, 1.0).astype(BF16)
        km_ref[hh] = jnp.where(own, means, 0.0).astype(BF16)

    units = [(i, hh) for i in range(nb - 1, -1, -1) for hh in range(HEADS_PER_TILE)]
    n = len(units)
    q_aug, penalty, tiles, col_max, probs, acc, outs = ({} for _ in range(7))

    def prep(u):
        i, hh = units[u]
        q_i = q_ref[0, i * ATT_BLOCK:(i + 1) * ATT_BLOCK, :]
        own_q = (lane_q >= hh * HEAD_DIM) & (lane_q < (hh + 1) * HEAD_DIM)
        q_aug[u] = jnp.where(own_q, q_i, qaux_ref[0, hh:hh + 1, :].astype(BF16))
        if i > MOBA_TOPK:
            gate = _dot_nt(km_ref[hh], q_i)[:SUBLANES]
            beaten = jnp.zeros((SUBLANES, ATT_BLOCK), F32)
            for jp in range(i):
                g_jp = gate[jp:jp + 1, :]
                wins = (g_jp > gate) | ((g_jp == gate) & (jp < blk_row))
                beaten = beaten + jnp.where(wins, 1.0, 0.0)
            chosen = (blk_row < i) & (beaten < MOBA_TOPK)
            penalty[u] = jnp.where(chosen, 0.0, NEG)

    def qk(u):
        i, hh = units[u]
        ts = []
        for j in range(i + 1):
            s = _dot_nt(kaug_ref[hh, j * ATT_BLOCK:(j + 1) * ATT_BLOCK, :], q_aug[u])
            if j == i:
                s = jnp.where(causal, s, NEG)
            elif i > MOBA_TOPK:
                s = s + penalty[u][j:j + 1, :]
            ts.append(s)
        m_el = ts[0]
        for s in ts[1:]:
            m_el = jnp.maximum(m_el, s)
        tiles[u] = ts
        col_max[u] = jnp.max(m_el, axis=0, keepdims=True)

    def expo(u):
        probs[u] = jnp.concatenate([jnp.exp(s - col_max[u]).astype(BF16) for s in tiles[u]], axis=0)

    def pv(u):
        i, hh = units[u]
        acc[u] = _dot(vaug_t_ref[hh, :, 0:(i + 1) * ATT_BLOCK], probs[u])

    def fin(u):
        i, hh = units[u]
        spare0 = (1 - hh) * HEAD_DIM
        outs[u] = acc[u] / acc[u][spare0:spare0 + 1, :]
        if hh == HEADS_PER_TILE - 1:
            o_pair_t = jnp.where(chan_t < HEAD_DIM, outs[u - 1], outs[u])
            o_ref[0, i * ATT_BLOCK:(i + 1) * ATT_BLOCK, :] = o_pair_t.T.astype(o_ref.dtype)

    stages = (prep, qk, expo, pv, fin)
    for t in range(n + (len(stages) - 1) * MOBA_SKEW):
        for k, stage in enumerate(stages):
            u = t - k * MOBA_SKEW
            if 0 <= u < n:
                stage(u)


def _moba_aux(seq):
    pos = jnp.arange(seq)
    lane = jnp.arange(LANES)
    kaux, qaux = [], []
    for hh in range(HEADS_PER_TILE):
        spare0 = (1 - hh) * HEAD_DIM
        j = lane[None, :] - spare0
        off = jnp.where(j == 0, (pos % MOBA_BLOCK)[:, None], 0).astype(F32)
        start = jnp.where(j == 1, ((pos // MOBA_BLOCK) * MOBA_BLOCK)[:, None], 0).astype(F32)
        kaux.append(off + start)
        qaux.append(((j == 0) | (j == 1))[0].astype(F32))
    kaux = jnp.stack(kaux).astype(BF16)
    qaux = jnp.stack(qaux)
    slopes = jnp.asarray([2.0 ** (-8.0 * (h + 1) / MOBA_HEADS) for h in range(MOBA_HEADS)], F32)
    slopes = slopes.reshape(MOBA_HEADS // HEADS_PER_TILE, HEADS_PER_TILE, 1)
    return kaux, qaux[None] * slopes


def _moba(proj3, q_col, k_col, v_col):
    b, seq, _ = proj3.shape
    nb = seq // MOBA_BLOCK
    assert seq % MOBA_BLOCK == 0 and nb <= SUBLANES
    pairs = MOBA_HEADS // HEADS_PER_TILE
    kaux, qaux = _moba_aux(seq)
    blk = lambda c0: pl.BlockSpec((1, seq, LANES), lambda bi, p: (bi, 0, c0 + p))
    return pl.pallas_call(
        _moba_kernel,
        out_shape=jax.ShapeDtypeStruct((b, seq, MIX_WIDTH), BF16),
        grid=(b, pairs),
        in_specs=[
            blk(q_col // LANES), blk(k_col // LANES), blk(v_col // LANES),
            _resident((HEADS_PER_TILE, seq, LANES)),
            pl.BlockSpec((1, HEADS_PER_TILE, LANES), lambda bi, p: (p, 0, 0)),
        ],
        out_specs=pl.BlockSpec((1, seq, LANES), lambda bi, p: (bi, 0, p)),
        scratch_shapes=[
            pltpu.VMEM((HEADS_PER_TILE, seq, LANES), BF16),
            pltpu.VMEM((HEADS_PER_TILE, LANES, seq), BF16),
            pltpu.VMEM((HEADS_PER_TILE, 2 * SUBLANES * (-(-nb // (2 * SUBLANES))), LANES), BF16),
        ],
        compiler_params=_params(2),
        name="moba",
    )(proj3, proj3, proj3, kaux, qaux)


SB_DEAD = 120.0
SB_SKEW = 2


def _sb_kernel(q_ref, k_ref, v_ref, tri_ref, o_ref, acc_ref, rest_ref):
    seq = q_ref.shape[1]
    nb = seq // ATT_BLOCK
    row_t = lax.broadcasted_iota(jnp.int32, (ATT_BLOCK, ATT_BLOCK), 0)
    col_t = lax.broadcasted_iota(jnp.int32, (ATT_BLOCK, ATT_BLOCK), 1)
    strict = col_t < row_t
    lane_q = lax.broadcasted_iota(jnp.int32, (ATT_BLOCK, LANES), 1)
    tri = tri_ref[...]
    key_rows = lambda j: slice(j * ATT_BLOCK, (j + 1) * ATT_BLOCK)

    def q_own(i, hh):
        q_i = q_ref[0, i * ATT_BLOCK:(i + 1) * ATT_BLOCK, :]
        own_q = (lane_q >= hh * HEAD_DIM) & (lane_q < (hh + 1) * HEAD_DIM)
        return jnp.where(own_q, q_i, jnp.zeros_like(q_i))

    def run_chains(chains, rest_of):
        n = len(chains)
        z, sp_bf, suffix, a = {}, {}, {}, {}
        outs, new_rests = [None] * n, [None] * n

        def scores(c):
            i, hh, j = chains[c]
            z[c] = _dot_nt(q_own(i, hh), k_ref[0, key_rows(j), :])

        def softplus(c):
            i, hh, j = chains[c]
            neg_abs = lax.bitcast_convert_type(
                lax.bitcast_convert_type(z[c], jnp.uint32) | jnp.uint32(0x80000000), F32)
            sp = jnp.maximum(z[c], 0.0) + jnp.log(1.0 + jnp.exp(neg_abs))
            if j == i:
                sp = jnp.where(strict, sp, 0.0)
            sp_bf[c] = sp.astype(BF16)

        def suffix_sum(c):
            suffix[c] = _dot(sp_bf[c], tri)

        def weights(c):
            i, hh, j = chains[c]
            rest = rest_of(c, new_rests)
            w = jnp.exp(z[c] - suffix[c] - rest)
            if j == i:
                w = jnp.where(strict, w, 0.0)
            a[c] = w.astype(BF16)
            new_rests[c] = rest + suffix[c][:, 0:1]

        def values(c):
            i, hh, j = chains[c]
            outs[c] = _dot(a[c], v_ref[0, key_rows(j), :])

        stages = (scores, softplus, suffix_sum, weights, values)
        for t in range(n + (len(stages) - 1) * SB_SKEW):
            for k, stage in enumerate(stages):
                c = t - k * SB_SKEW
                if 0 <= c < n:
                    stage(c)
        return outs, new_rests

    slot = lambda i, hh: i * HEADS_PER_TILE + hh
    units = lambda d: [(i, hh, i - d) for i in range(d, nb) for hh in range(HEADS_PER_TILE)]

    near = units(0) + (units(1) if nb > 1 else [])
    where = {unit: c for c, unit in enumerate(near)}
    zero_rest = jnp.zeros((ATT_BLOCK, 1), F32)

    def near_rest(c, done):
        i, hh, j = near[c]
        return zero_rest if j == i else done[where[(i, hh, i)]]

    outs, rests = run_chains(near, near_rest)
    for (i, hh, j), o, r in zip(near, outs, rests):
        if j == i and (i, hh, i - 1) in where:
            continue
        total = o if j == i else o + outs[where[(i, hh, i)]]
        acc_ref[slot(i, hh)] = total
        rest_ref[slot(i, hh)] = jnp.broadcast_to(r, (ATT_BLOCK, LANES))

    def alive(d):
        low = rest_ref[slot(d, 0)]
        for s in range(slot(d, 0) + 1, nb * HEADS_PER_TILE):
            low = jnp.minimum(low, rest_ref[s])
        return jnp.min(low) < SB_DEAD

    def far_diagonal(d):
        far = units(d)
        outs, rests = run_chains(far, lambda c, _: rest_ref[slot(far[c][0], far[c][1])][:, 0:1])
        for (i, hh, _), o, r in zip(far, outs, rests):
            acc_ref[slot(i, hh)] = acc_ref[slot(i, hh)] + o
            rest_ref[slot(i, hh)] = jnp.broadcast_to(r, (ATT_BLOCK, LANES))

    def from_diagonal(d):
        far_diagonal(d)
        if d + 1 < nb:
            pl.when(alive(d + 1))(functools.partial(from_diagonal, d + 1))

    if nb > 2:
        pl.when(alive(2))(functools.partial(from_diagonal, 2))

    for i in range(nb):
        o_pair = jnp.where(lane_q < HEAD_DIM, acc_ref[slot(i, 0)], acc_ref[slot(i, 1)])
        o_ref[0, i * ATT_BLOCK:(i + 1) * ATT_BLOCK, :] = o_pair.astype(o_ref.dtype)


def _stick_breaking(proj3, q_col, k_col, v_col):
    b, seq, _ = proj3.shape
    assert seq % ATT_BLOCK == 0
    nb = seq // ATT_BLOCK
    pairs = SB_HEADS // HEADS_PER_TILE
    idx = jnp.arange(ATT_BLOCK)
    tri = (idx[:, None] >= idx[None, :]).astype(BF16)
    blk = lambda c0: pl.BlockSpec((1, seq, LANES), lambda bi, p: (bi, 0, c0 + p))
    state = pltpu.VMEM((nb * HEADS_PER_TILE, ATT_BLOCK, LANES), F32)
    return pl.pallas_call(
        _sb_kernel,
        out_shape=jax.ShapeDtypeStruct((b, seq, MIX_WIDTH), BF16),
        grid=(b, pairs),
        in_specs=[
            blk(q_col // LANES), blk(k_col // LANES), blk(v_col // LANES),
            _resident(tri.shape),
        ],
        out_specs=pl.BlockSpec((1, seq, LANES), lambda bi, p: (bi, 0, p)),
        scratch_shapes=[state, state],
        compiler_params=_params(2),
        name="stick_breaking",
    )(proj3, proj3, proj3, tri)


def _mem_kv_kernel(mem_ref, g_ref, w_ref, o_ref):
    h = _rms_scale(mem_ref[0], g_ref[...]).astype(BF16)
    o_ref[0] = _dot(h, w_ref[...]).astype(o_ref.dtype)


def _mem_kv(mem, g, w):
    b, m, d = mem.shape
    return pl.pallas_call(
        _mem_kv_kernel,
        out_shape=jax.ShapeDtypeStruct((b, m, w.shape[1]), BF16),
        grid=(b,),
        in_specs=[pl.BlockSpec((1, m, d), lambda i: (i, 0, 0)), _resident((1, d)), _resident(w.shape)],
        out_specs=pl.BlockSpec((1, m, w.shape[1]), lambda i: (i, 0, 0)),
        compiler_params=_params(1),
        name="mem_kv",
    )(mem, g, w)


def _mix_cross_kernel(a_ref, b_ref, ga_ref, gb_ref, x_ref, kv_ref, g_ref,
                      wa_ref, wb_ref, wo_ref, wq_ref, wxo_ref, o_ref):
    half = x_ref.shape[1] // 2
    rows = (slice(0, half), slice(half, 2 * half))
    ya, yb, merged, x1, q, probs, denom, attn = ({} for _ in range(8))
    head_cols = lambda hd: slice(hd * X_HEAD_DIM, (hd + 1) * X_HEAD_DIM)

    def branches(r):
        ya[r] = _dot(a_ref[0, rows[r], :], wa_ref[...])
        yb[r] = _dot(b_ref[0, rows[r], :], wb_ref[...])

    def merge(r):
        m = ga_ref[0, rows[r], :].astype(F32) * ya[r] + gb_ref[0, rows[r], :].astype(F32) * yb[r]
        merged[r] = m.astype(BF16)

    def mix_residual(r):
        x1[r] = x_ref[0, rows[r], :] + _dot(merged[r], wo_ref[...])

    def project(r):
        h = _rms_scale(x1[r], g_ref[...]).astype(BF16)
        q[r] = (_dot(h, wq_ref[...]) * (X_HEAD_DIM ** -0.5)).astype(BF16)

    def scores(r):
        probs[r], denom[r] = [], []
        for hd in range(X_HEADS):
            s = _dot_nt(q[r][:, head_cols(hd)], kv_ref[0, :, head_cols(hd)])
            p = jnp.exp(s - jnp.max(s, axis=1, keepdims=True))
            denom[r].append(jnp.sum(p, axis=1, keepdims=True))
            probs[r].append(p.astype(BF16))

    def values(r):
        heads = []
        for hd in range(X_HEADS):
            v_h = kv_ref[0, :, X_WIDTH + hd * X_HEAD_DIM:X_WIDTH + (hd + 1) * X_HEAD_DIM]
            heads.append((_dot(probs[r][hd], v_h) / denom[r][hd]).astype(BF16))
        attn[r] = jnp.concatenate(heads, axis=1)

    def cross_residual(r):
        o_ref[0, rows[r], :] = x1[r] + _dot(attn[r], wxo_ref[...])

    stages = (branches, merge, mix_residual, project, scores, values, cross_residual)
    for t in range(len(rows) + len(stages) - 1):
        for k, stage in enumerate(stages):
            if 0 <= t - k < len(rows):
                stage(t - k)


def _mix_cross(moba_o, sb_o, proj3, gate_col, x3, kv, g, wa, wb, wo, wq, wxo):
    b, seq, d = x3.shape
    m = kv.shape[1]
    rows = lambda w, c=0: pl.BlockSpec((1, ROW_TILE, w), lambda bi, i: (bi, i, c))
    return pl.pallas_call(
        _mix_cross_kernel,
        out_shape=jax.ShapeDtypeStruct((b, seq, d), F32),
        grid=(b, seq // ROW_TILE),
        in_specs=[
            rows(MIX_WIDTH), rows(MIX_WIDTH),
            rows(d, gate_col // d), rows(d, gate_col // d + 1),
            rows(d),
            pl.BlockSpec((1, m, kv.shape[2]), lambda bi, i: (bi, 0, 0)),
            _resident((1, d)),
            _resident(wa.shape), _resident(wb.shape), _resident(wo.shape),
            _resident(wq.shape), _resident(wxo.shape),
        ],
        out_specs=rows(d),
        compiler_params=_params(2),
        name="mix_cross",
    )(moba_o, sb_o, proj3, proj3, x3, kv, g, wa, wb, wo, wq, wxo)


def _mlp_kernel(x_ref, g_ref, wu_ref, wd_ref, gf_ref, o_ref, *, final_norm):
    x = x_ref[...]
    h = _rms_scale(x, g_ref[...]).astype(BF16)
    acc = x
    for c in range(wu_ref.shape[1] // COL_CHUNK):
        cols = slice(c * COL_CHUNK, (c + 1) * COL_CHUNK)
        u = jnp.maximum(_dot(h, wu_ref[:, cols]), 0.0)
        acc = acc + _dot((u * u).astype(BF16), wd_ref[cols, :])
    o_ref[...] = _rms_scale(acc, gf_ref[...]) if final_norm else acc


def _mlp(x2d, g, wu, wd, gf, final_norm):
    n, d = x2d.shape
    return pl.pallas_call(
        functools.partial(_mlp_kernel, final_norm=final_norm),
        out_shape=jax.ShapeDtypeStruct((n, d), F32),
        grid=(n // ROW_TILE,),
        in_specs=[
            pl.BlockSpec((ROW_TILE, d), lambda i: (i, 0)),
            _resident((1, d)), _resident(wu.shape), _resident(wd.shape), _resident((1, d)),
        ],
        out_specs=pl.BlockSpec((ROW_TILE, d), lambda i: (i, 0)),
        compiler_params=_params(1),
        name="mlp_final",
    )(x2d, g, wu, wd, gf)


def kernel(x, mem, g_mix, w_in, w_br_moba, w_br_sb, w_out, g_cross, g_mem,
           w_xq, w_xkv, w_xo, g_mlp, w_up, w_down, g_final):
    b, seq, d = x.shape
    n = b * seq
    assert n % ROW_TILE == 0 and seq % ROW_TILE == 0
    row = lambda v: v.reshape(1, -1).astype(F32)
    depth = g_mix.shape[0]
    x2d = x.reshape(n, d)
    for l in range(depth):
        proj = _in_proj(x2d, row(g_mix[l]), w_in[l].astype(BF16))
        proj3 = proj.reshape(b, seq, -1)
        moba_o = _moba(proj3, 0, MIX_WIDTH, 2 * MIX_WIDTH)
        sb_o = _stick_breaking(proj3, 3 * MIX_WIDTH, 4 * MIX_WIDTH, 5 * MIX_WIDTH)
        kv = _mem_kv(mem, row(g_mem[l]), w_xkv[l].astype(BF16))
        x3 = _mix_cross(moba_o, sb_o, proj3, 6 * MIX_WIDTH, x2d.reshape(b, seq, d), kv, row(g_cross[l]),
                        w_br_moba[l].astype(BF16), w_br_sb[l].astype(BF16), w_out[l].astype(BF16),
                        w_xq[l].astype(BF16), w_xo[l].astype(BF16))
        x2d = _mlp(x3.reshape(n, d), row(g_mlp[l]), w_up[l].astype(BF16), w_down[l].astype(BF16),
                   row(g_final), final_norm=(l == depth - 1))
    return x2d.reshape(b, seq, d)
```

```python
import functools
import math

import jax
import jax.numpy as jnp
import numpy as np
from jax import lax
from jax.experimental import pallas as pl
from jax.experimental.pallas import tpu as pltpu

F32 = jnp.float32
BF16 = jnp.bfloat16

HEAD_DIM = 64
MOBA_HEADS = 8
SB_HEADS = 8
MIX_WIDTH = MOBA_HEADS * HEAD_DIM
MOBA_BLOCK = 256
MOBA_TOPK = 3
X_HEADS = 4
X_HEAD_DIM = 128
X_WIDTH = X_HEADS * X_HEAD_DIM
RMS_EPS = 1e-6
NEG = -1e30
LOG2E = math.log2(math.e)
SLOPE_TERMS = 4

LANES = 128
SUBLANES = 8
HEADS_PER_TILE = LANES // HEAD_DIM
ATT_BLOCK = 256
ROW_TILE = 512
COL_CHUNK = 512
VMEM_LIMIT = 56 * 1024 * 1024


def _params(n_axes):
    return pltpu.CompilerParams(
        dimension_semantics=("arbitrary",) * n_axes, vmem_limit_bytes=VMEM_LIMIT)


def _resident(shape):
    return pl.BlockSpec(shape, lambda *_: (0,) * len(shape), pipeline_mode=pl.Buffered(1))


def _rms_scale(x, g):
    y = x * lax.rsqrt(jnp.mean(x * x, axis=-1, keepdims=True) + RMS_EPS)
    return y * g


def _dot(a, b):
    return jnp.dot(a, b, preferred_element_type=F32)


def _dot_nt(a, b):
    return lax.dot_general(a, b, (((1,), (1,)), ((), ())), preferred_element_type=F32)


def _in_proj_kernel(x_ref, g_ref, w_ref, o_ref, *, q_chunks, gate_start):
    n_chunks = w_ref.shape[1] // COL_CHUNK
    half = x_ref.shape[0] // 2
    rows = (slice(0, half), slice(half, 2 * half))
    h = {}

    def norm(r):
        h[r] = _rms_scale(x_ref[rows[r], :], g_ref[...]).astype(BF16)

    def chunk(r, c):
        cols = slice(c * COL_CHUNK, (c + 1) * COL_CHUNK)
        acc = _dot(h[r], w_ref[:, cols])
        if c in q_chunks:
            acc = acc * (HEAD_DIM ** -0.5 * LOG2E)
        if c >= gate_start:
            acc = 1.0 / (1.0 + jnp.exp(-acc))
        o_ref[rows[r], cols] = acc.astype(o_ref.dtype)

    norm(0)
    chunk(0, 0)
    norm(1)
    for c in range(1, n_chunks):
        chunk(1, c - 1)
        chunk(0, c)
    chunk(1, n_chunks - 1)


def _in_proj(x2d, g, w):
    n, d = x2d.shape
    cols = w.shape[1]
    q_chunks = (0, 3 * MIX_WIDTH // COL_CHUNK)
    gate_start = 6 * MIX_WIDTH // COL_CHUNK
    return pl.pallas_call(
        functools.partial(_in_proj_kernel, q_chunks=q_chunks, gate_start=gate_start),
        out_shape=jax.ShapeDtypeStruct((n, cols), BF16),
        grid=(n // ROW_TILE,),
        in_specs=[
            pl.BlockSpec((ROW_TILE, d), lambda i: (i, 0)),
            _resident((1, d)),
            _resident((d, cols)),
        ],
        out_specs=pl.BlockSpec((ROW_TILE, cols), lambda i: (i, 0)),
        compiler_params=_params(1),
        name="in_proj",
    )(x2d, g, w)


MOBA_SKEW = 1


def _moba_kernel(q_ref, k_ref, v_ref, kaux_ref, qaux_ref, o_ref,
                 kaug_ref, vaug_t_ref, km_ref):
    seq = q_ref.shape[1]
    nb = seq // ATT_BLOCK
    lane = lax.broadcasted_iota(jnp.int32, (1, LANES), 1)
    key_t = lax.broadcasted_iota(jnp.int32, (ATT_BLOCK, ATT_BLOCK), 0)
    qry_t = lax.broadcasted_iota(jnp.int32, (ATT_BLOCK, ATT_BLOCK), 1)
    causal = key_t <= qry_t
    lane_q = lax.broadcasted_iota(jnp.int32, (ATT_BLOCK, LANES), 1)
    chan_t = lax.broadcasted_iota(jnp.int32, (LANES, ATT_BLOCK), 0)
    nb_rows = km_ref.shape[1]
    blk_row = lax.broadcasted_iota(jnp.int32, (SUBLANES, ATT_BLOCK), 0)

    k_all = k_ref[0]
    v_all = v_ref[0]
    means = [jnp.mean(k_all[j * ATT_BLOCK:(j + 1) * ATT_BLOCK].astype(F32), axis=0, keepdims=True)
             for j in range(nb)]
    means.append(jnp.zeros((nb_rows - nb, LANES), F32))
    means = jnp.concatenate(means, axis=0)
    v_t = v_all.astype(F32).T
    chan_all = lax.broadcasted_iota(jnp.int32, (LANES, seq), 0)
    for hh in range(HEADS_PER_TILE):
        own = (lane >= hh * HEAD_DIM) & (lane < (hh + 1) * HEAD_DIM)
        own_t = (chan_all >= hh * HEAD_DIM) & (chan_all < (hh + 1) * HEAD_DIM)
        kaug_ref[hh] = jnp.where(own, k_all, kaux_ref[hh])
        vaug_t_ref[hh] = jnp.where(own_t, v_t, 1.0).astype(BF16)
        km_ref[hh] = jnp.where(own, means, 0.0).astype(BF16)

    units = [(i, hh) for i in range(nb - 1, -1, -1) for hh in range(HEADS_PER_TILE)]
    n = len(units)
    q_aug, penalty, tiles, col_max, probs, acc, outs = ({} for _ in range(7))

    def prep(u):
        i, hh = units[u]
        q_i = q_ref[0, i * ATT_BLOCK:(i + 1) * ATT_BLOCK, :]
        own_q = (lane_q >= hh * HEAD_DIM) & (lane_q < (hh + 1) * HEAD_DIM)
        q_aug[u] = jnp.where(own_q, q_i, qaux_ref[0, hh:hh + 1, :].astype(BF16))
        if i > MOBA_TOPK:
            gate = _dot_nt(km_ref[hh], q_i)[:SUBLANES]
            beaten = jnp.zeros((SUBLANES, ATT_BLOCK), F32)
            for jp in range(i):
                g_jp = gate[jp:jp + 1, :]
                wins = (g_jp > gate) | ((g_jp == gate) & (jp < blk_row))
                beaten = beaten + jnp.where(wins, 1.0, 0.0)
            chosen = (blk_row < i) & (beaten < MOBA_TOPK)
            penalty[u] = jnp.where(chosen, 0.0, NEG)

    def qk(u):
        i, hh = units[u]
        ts = []
        for j in range(i + 1):
            s = _dot_nt(kaug_ref[hh, j * ATT_BLOCK:(j + 1) * ATT_BLOCK, :], q_aug[u])
            if j == i:
                s = jnp.where(causal, s, NEG)
            elif i > MOBA_TOPK:
                s = s + penalty[u][j:j + 1, :]
            ts.append(s)
        m_el = ts[0]
        for s in ts[1:]:
            m_el = jnp.maximum(m_el, s)
        tiles[u] = ts
        col_max[u] = jnp.max(m_el, axis=0, keepdims=True)

    def expo(u):
        probs[u] = jnp.concatenate([jnp.exp2(s - col_max[u]).astype(BF16) for s in tiles[u]], axis=0)

    def pv(u):
        i, hh = units[u]
        acc[u] = _dot(vaug_t_ref[hh, :, 0:(i + 1) * ATT_BLOCK], probs[u])

    def fin(u):
        i, hh = units[u]
        spare0 = (1 - hh) * HEAD_DIM
        outs[u] = acc[u] / acc[u][spare0:spare0 + 1, :]
        if hh == HEADS_PER_TILE - 1:
            o_pair_t = jnp.where(chan_t < HEAD_DIM, outs[u - 1], outs[u])
            o_ref[0, i * ATT_BLOCK:(i + 1) * ATT_BLOCK, :] = o_pair_t.T.astype(o_ref.dtype)

    stages = (prep, qk, expo, pv, fin)
    for t in range(n + (len(stages) - 1) * MOBA_SKEW):
        for k, stage in enumerate(stages):
            u = t - k * MOBA_SKEW
            if 0 <= u < n:
                stage(u)


def _bf16_terms(value, n_terms):
    terms, rest = [], float(value)
    for _ in range(n_terms):
        term = float(np.asarray(rest, np.float32).astype(jnp.bfloat16).astype(np.float32))
        terms.append(term)
        rest -= term
    return terms


def _moba_aux(seq):
    pos = np.arange(seq)
    lane = np.arange(LANES)
    pairs = MOBA_HEADS // HEADS_PER_TILE
    kaux = np.zeros((HEADS_PER_TILE, seq, LANES), np.float32)
    qaux = np.zeros((pairs, HEADS_PER_TILE, LANES), np.float32)
    for hh in range(HEADS_PER_TILE):
        j = lane - (1 - hh) * HEAD_DIM
        off_lanes = (j >= 0) & (j < SLOPE_TERMS)
        start_lanes = (j >= SLOPE_TERMS) & (j < 2 * SLOPE_TERMS)
        kaux[hh] = (np.where(off_lanes[None, :], (pos % MOBA_BLOCK)[:, None], 0)
                    + np.where(start_lanes[None, :], ((pos // MOBA_BLOCK) * MOBA_BLOCK)[:, None], 0))
        for p in range(pairs):
            slope = 2.0 ** (-8.0 * (p * HEADS_PER_TILE + hh + 1) / MOBA_HEADS)
            terms = _bf16_terms(slope * LOG2E, SLOPE_TERMS)
            for t, term in enumerate(terms):
                qaux[p, hh, (j == t) | (j == t + SLOPE_TERMS)] = term
    return jnp.asarray(kaux, BF16), jnp.asarray(qaux)


def _moba(proj3, q_col, k_col, v_col):
    b, seq, _ = proj3.shape
    nb = seq // MOBA_BLOCK
    assert seq % MOBA_BLOCK == 0 and nb <= SUBLANES
    pairs = MOBA_HEADS // HEADS_PER_TILE
    kaux, qaux = _moba_aux(seq)
    blk = lambda c0: pl.BlockSpec((1, seq, LANES), lambda bi, p: (bi, 0, c0 + p))
    return pl.pallas_call(
        _moba_kernel,
        out_shape=jax.ShapeDtypeStruct((b, seq, MIX_WIDTH), BF16),
        grid=(b, pairs),
        in_specs=[
            blk(q_col // LANES), blk(k_col // LANES), blk(v_col // LANES),
            _resident((HEADS_PER_TILE, seq, LANES)),
            pl.BlockSpec((1, HEADS_PER_TILE, LANES), lambda bi, p: (p, 0, 0)),
        ],
        out_specs=pl.BlockSpec((1, seq, LANES), lambda bi, p: (bi, 0, p)),
        scratch_shapes=[
            pltpu.VMEM((HEADS_PER_TILE, seq, LANES), BF16),
            pltpu.VMEM((HEADS_PER_TILE, LANES, seq), BF16),
            pltpu.VMEM((HEADS_PER_TILE, 2 * SUBLANES * (-(-nb // (2 * SUBLANES))), LANES), BF16),
        ],
        compiler_params=_params(2),
        name="moba",
    )(proj3, proj3, proj3, kaux, qaux)


SB_DEAD = 160.0
SB_SKEW = 2


def _sb_kernel(q_ref, k_ref, v_ref, tri_ref, o_ref, acc_ref, rest_ref):
    seq = q_ref.shape[1]
    nb = seq // ATT_BLOCK
    row_t = lax.broadcasted_iota(jnp.int32, (ATT_BLOCK, ATT_BLOCK), 0)
    col_t = lax.broadcasted_iota(jnp.int32, (ATT_BLOCK, ATT_BLOCK), 1)
    strict = col_t < row_t
    lane_q = lax.broadcasted_iota(jnp.int32, (ATT_BLOCK, LANES), 1)
    tri = tri_ref[...]
    key_rows = lambda j: slice(j * ATT_BLOCK, (j + 1) * ATT_BLOCK)

    def q_own(i, hh):
        q_i = q_ref[0, i * ATT_BLOCK:(i + 1) * ATT_BLOCK, :]
        own_q = (lane_q >= hh * HEAD_DIM) & (lane_q < (hh + 1) * HEAD_DIM)
        return jnp.where(own_q, q_i, jnp.zeros_like(q_i))

    def run_chains(chains, rest_of):
        n = len(chains)
        z, sp_bf, suffix, a = {}, {}, {}, {}
        outs, new_rests = [None] * n, [None] * n

        def scores(c):
            i, hh, j = chains[c]
            z[c] = _dot_nt(q_own(i, hh), k_ref[0, key_rows(j), :])

        def softplus(c):
            i, hh, j = chains[c]
            neg_abs = lax.bitcast_convert_type(
                lax.bitcast_convert_type(z[c], jnp.uint32) | jnp.uint32(0x80000000), F32)
            sp = jnp.maximum(z[c], 0.0) + jnp.log(1.0 + jnp.exp2(neg_abs)) * LOG2E
            if j == i:
                sp = jnp.where(strict, sp, 0.0)
            sp_bf[c] = sp.astype(BF16)

        def suffix_sum(c):
            suffix[c] = _dot(sp_bf[c], tri)

        def weights(c):
            i, hh, j = chains[c]
            rest = rest_of(c, new_rests)
            w = jnp.exp2(z[c] - suffix[c] - rest)
            if j == i:
                w = jnp.where(strict, w, 0.0)
            a[c] = w.astype(BF16)
            new_rests[c] = rest + suffix[c][:, 0:1]

        def values(c):
            i, hh, j = chains[c]
            outs[c] = _dot(a[c], v_ref[0, key_rows(j), :])

        stages = (scores, softplus, suffix_sum, weights, values)
        for t in range(n + (len(stages) - 1) * SB_SKEW):
            for k, stage in enumerate(stages):
                c = t - k * SB_SKEW
                if 0 <= c < n:
                    stage(c)
        return outs, new_rests

    slot = lambda i, hh: i * HEADS_PER_TILE + hh
    units = lambda d: [(i, hh, i - d) for i in range(d, nb) for hh in range(HEADS_PER_TILE)]

    near = units(0) + (units(1) if nb > 1 else [])
    where = {unit: c for c, unit in enumerate(near)}
    zero_rest = jnp.zeros((ATT_BLOCK, 1), F32)

    def near_rest(c, done):
        i, hh, j = near[c]
        return zero_rest if j == i else done[where[(i, hh, i)]]

    outs, rests = run_chains(near, near_rest)
    for (i, hh, j), o, r in zip(near, outs, rests):
        if j == i and (i, hh, i - 1) in where:
            continue
        total = o if j == i else o + outs[where[(i, hh, i)]]
        acc_ref[slot(i, hh)] = total
        rest_ref[slot(i, hh)] = jnp.broadcast_to(r, (ATT_BLOCK, LANES))

    def alive(d):
        low = rest_ref[slot(d, 0)]
        for s in range(slot(d, 0) + 1, nb * HEADS_PER_TILE):
            low = jnp.minimum(low, rest_ref[s])
        return jnp.min(low) < SB_DEAD

    def far_diagonal(d):
        far = units(d)
        outs, rests = run_chains(far, lambda c, _: rest_ref[slot(far[c][0], far[c][1])][:, 0:1])
        for (i, hh, _), o, r in zip(far, outs, rests):
            acc_ref[slot(i, hh)] = acc_ref[slot(i, hh)] + o
            rest_ref[slot(i, hh)] = jnp.broadcast_to(r, (ATT_BLOCK, LANES))

    def from_diagonal(d):
        far_diagonal(d)
        if d + 1 < nb:
            pl.when(alive(d + 1))(functools.partial(from_diagonal, d + 1))

    if nb > 2:
        pl.when(alive(2))(functools.partial(from_diagonal, 2))

    for i in range(nb):
        o_pair = jnp.where(lane_q < HEAD_DIM, acc_ref[slot(i, 0)], acc_ref[slot(i, 1)])
        o_ref[0, i * ATT_BLOCK:(i + 1) * ATT_BLOCK, :] = o_pair.astype(o_ref.dtype)


def _stick_breaking(proj3, q_col, k_col, v_col):
    b, seq, _ = proj3.shape
    assert seq % ATT_BLOCK == 0
    nb = seq // ATT_BLOCK
    pairs = SB_HEADS // HEADS_PER_TILE
    idx = jnp.arange(ATT_BLOCK)
    tri = (idx[:, None] >= idx[None, :]).astype(BF16)
    blk = lambda c0: pl.BlockSpec((1, seq, LANES), lambda bi, p: (bi, 0, c0 + p))
    state = pltpu.VMEM((nb * HEADS_PER_TILE, ATT_BLOCK, LANES), F32)
    return pl.pallas_call(
        _sb_kernel,
        out_shape=jax.ShapeDtypeStruct((b, seq, MIX_WIDTH), BF16),
        grid=(b, pairs),
        in_specs=[
            blk(q_col // LANES), blk(k_col // LANES), blk(v_col // LANES),
            _resident(tri.shape),
        ],
        out_specs=pl.BlockSpec((1, seq, LANES), lambda bi, p: (bi, 0, p)),
        scratch_shapes=[state, state],
        compiler_params=_params(2),
        name="stick_breaking",
    )(proj3, proj3, proj3, tri)


def _mem_kv_kernel(mem_ref, g_ref, w_ref, o_ref):
    h = _rms_scale(mem_ref[0], g_ref[...]).astype(BF16)
    o_ref[0] = _dot(h, w_ref[...]).astype(o_ref.dtype)


def _mem_kv(mem, g, w):
    b, m, d = mem.shape
    return pl.pallas_call(
        _mem_kv_kernel,
        out_shape=jax.ShapeDtypeStruct((b, m, w.shape[1]), BF16),
        grid=(b,),
        in_specs=[pl.BlockSpec((1, m, d), lambda i: (i, 0, 0)), _resident((1, d)), _resident(w.shape)],
        out_specs=pl.BlockSpec((1, m, w.shape[1]), lambda i: (i, 0, 0)),
        compiler_params=_params(1),
        name="mem_kv",
    )(mem, g, w)


def _mix_cross_kernel(a_ref, b_ref, ga_ref, gb_ref, x_ref, kv_ref, g_ref,
                      wa_ref, wb_ref, wo_ref, wq_ref, wxo_ref, o_ref):
    half = x_ref.shape[1] // 2
    rows = (slice(0, half), slice(half, 2 * half))
    ya, yb, merged, x1, q, probs, denom, attn = ({} for _ in range(8))
    head_cols = lambda hd: slice(hd * X_HEAD_DIM, (hd + 1) * X_HEAD_DIM)

    def branches(r):
        ya[r] = _dot(a_ref[0, rows[r], :], wa_ref[...])
        yb[r] = _dot(b_ref[0, rows[r], :], wb_ref[...])

    def merge(r):
        m = ga_ref[0, rows[r], :].astype(F32) * ya[r] + gb_ref[0, rows[r], :].astype(F32) * yb[r]
        merged[r] = m.astype(BF16)

    def mix_residual(r):
        x1[r] = x_ref[0, rows[r], :] + _dot(merged[r], wo_ref[...])

    def project(r):
        h = _rms_scale(x1[r], g_ref[...]).astype(BF16)
        q[r] = (_dot(h, wq_ref[...]) * (X_HEAD_DIM ** -0.5)).astype(BF16)

    def scores(r):
        probs[r], denom[r] = [], []
        for hd in range(X_HEADS):
            s = _dot_nt(q[r][:, head_cols(hd)], kv_ref[0, :, head_cols(hd)])
            p = jnp.exp(s - jnp.max(s, axis=1, keepdims=True))
            denom[r].append(jnp.sum(p, axis=1, keepdims=True))
            probs[r].append(p.astype(BF16))

    def values(r):
        heads = []
        for hd in range(X_HEADS):
            v_h = kv_ref[0, :, X_WIDTH + hd * X_HEAD_DIM:X_WIDTH + (hd + 1) * X_HEAD_DIM]
            heads.append((_dot(probs[r][hd], v_h) / denom[r][hd]).astype(BF16))
        attn[r] = jnp.concatenate(heads, axis=1)

    def cross_residual(r):
        o_ref[0, rows[r], :] = x1[r] + _dot(attn[r], wxo_ref[...])

    stages = (branches, merge, mix_residual, project, scores, values, cross_residual)
    for t in range(len(rows) + len(stages) - 1):
        for k, stage in enumerate(stages):
            if 0 <= t - k < len(rows):
                stage(t - k)


def _mix_cross(moba_o, sb_o, proj3, gate_col, x3, kv, g, wa, wb, wo, wq, wxo):
    b, seq, d = x3.shape
    m = kv.shape[1]
    rows = lambda w, c=0: pl.BlockSpec((1, ROW_TILE, w), lambda bi, i: (bi, i, c))
    return pl.pallas_call(
        _mix_cross_kernel,
        out_shape=jax.ShapeDtypeStruct((b, seq, d), F32),
        grid=(b, seq // ROW_TILE),
        in_specs=[
            rows(MIX_WIDTH), rows(MIX_WIDTH),
            rows(d, gate_col // d), rows(d, gate_col // d + 1),
            rows(d),
            pl.BlockSpec((1, m, kv.shape[2]), lambda bi, i: (bi, 0, 0)),
            _resident((1, d)),
            _resident(wa.shape), _resident(wb.shape), _resident(wo.shape),
            _resident(wq.shape), _resident(wxo.shape),
        ],
        out_specs=rows(d),
        compiler_params=_params(2),
        name="mix_cross",
    )(moba_o, sb_o, proj3, proj3, x3, kv, g, wa, wb, wo, wq, wxo)


def _mlp_kernel(x_ref, g_ref, wu_ref, wd_ref, gf_ref, o_ref, *, final_norm):
    x = x_ref[...]
    h = _rms_scale(x, g_ref[...]).astype(BF16)
    acc = x
    for c in range(wu_ref.shape[1] // COL_CHUNK):
        cols = slice(c * COL_CHUNK, (c + 1) * COL_CHUNK)
        u = jnp.maximum(_dot(h, wu_ref[:, cols]), 0.0)
        acc = acc + _dot((u * u).astype(BF16), wd_ref[cols, :])
    o_ref[...] = _rms_scale(acc, gf_ref[...]) if final_norm else acc


def _mlp(x2d, g, wu, wd, gf, final_norm):
    n, d = x2d.shape
    return pl.pallas_call(
        functools.partial(_mlp_kernel, final_norm=final_norm),
        out_shape=jax.ShapeDtypeStruct((n, d), F32),
        grid=(n // ROW_TILE,),
        in_specs=[
            pl.BlockSpec((ROW_TILE, d), lambda i: (i, 0)),
            _resident((1, d)), _resident(wu.shape), _resident(wd.shape), _resident((1, d)),
        ],
        out_specs=pl.BlockSpec((ROW_TILE, d), lambda i: (i, 0)),
        compiler_params=_params(1),
        name="mlp_final",
    )(x2d, g, wu, wd, gf)


def kernel(x, mem, g_mix, w_in, w_br_moba, w_br_sb, w_out, g_cross, g_mem,
           w_xq, w_xkv, w_xo, g_mlp, w_up, w_down, g_final):
    b, seq, d = x.shape
    n = b * seq
    assert n % ROW_TILE == 0 and seq % ROW_TILE == 0
    row = lambda v: v.reshape(1, -1).astype(F32)
    depth = g_mix.shape[0]
    x2d = x.reshape(n, d)
    for l in range(depth):
        proj = _in_proj(x2d, row(g_mix[l]), w_in[l].astype(BF16))
        proj3 = proj.reshape(b, seq, -1)
        moba_o = _moba(proj3, 0, MIX_WIDTH, 2 * MIX_WIDTH)
        sb_o = _stick_breaking(proj3, 3 * MIX_WIDTH, 4 * MIX_WIDTH, 5 * MIX_WIDTH)
        kv = _mem_kv(mem, row(g_mem[l]), w_xkv[l].astype(BF16))
        x3 = _mix_cross(moba_o, sb_o, proj3, 6 * MIX_WIDTH, x2d.reshape(b, seq, d), kv, row(g_cross[l]),
                        w_br_moba[l].astype(BF16), w_br_sb[l].astype(BF16), w_out[l].astype(BF16),
                        w_xq[l].astype(BF16), w_xo[l].astype(BF16))
        x2d = _mlp(x3.reshape(n, d), row(g_mlp[l]), w_up[l].astype(BF16), w_down[l].astype(BF16),
                   row(g_final), final_norm=(l == depth - 1))
    return x2d.reshape(b, seq, d)
```

```python
import functools
import math

import jax
import jax.numpy as jnp
import numpy as np
from jax import lax
from jax.experimental import pallas as pl
from jax.experimental.pallas import tpu as pltpu

F32 = jnp.float32
BF16 = jnp.bfloat16

HEAD_DIM = 64
MOBA_HEADS = 8
SB_HEADS = 8
MIX_WIDTH = MOBA_HEADS * HEAD_DIM
MOBA_BLOCK = 256
MOBA_TOPK = 3
X_HEADS = 4
X_HEAD_DIM = 128
X_WIDTH = X_HEADS * X_HEAD_DIM
RMS_EPS = 1e-6
NEG = -1e30
LOG2E = math.log2(math.e)
SLOPE_TERMS = 4

LANES = 128
SUBLANES = 8
HEADS_PER_TILE = LANES // HEAD_DIM
ATT_BLOCK = 256
ROW_TILE = 512
COL_CHUNK = 512
VMEM_LIMIT = 56 * 1024 * 1024


def _params(n_axes):
    return pltpu.CompilerParams(
        dimension_semantics=("arbitrary",) * n_axes, vmem_limit_bytes=VMEM_LIMIT)


def _resident(shape):
    return pl.BlockSpec(shape, lambda *_: (0,) * len(shape), pipeline_mode=pl.Buffered(1))


def _rms_scale(x, g):
    y = x * lax.rsqrt(jnp.mean(x * x, axis=-1, keepdims=True) + RMS_EPS)
    return y * g


def _dot(a, b):
    return jnp.dot(a, b, preferred_element_type=F32)


def _dot_nt(a, b):
    return lax.dot_general(a, b, (((1,), (1,)), ((), ())), preferred_element_type=F32)


def _in_proj_kernel(x_ref, g_ref, w_ref, o_ref, *, q_chunks, gate_start):
    n_chunks = w_ref.shape[1] // COL_CHUNK
    half = x_ref.shape[0] // 2
    rows = (slice(0, half), slice(half, 2 * half))
    h = {}

    def norm(r):
        h[r] = _rms_scale(x_ref[rows[r], :], g_ref[...]).astype(BF16)

    def chunk(r, c):
        cols = slice(c * COL_CHUNK, (c + 1) * COL_CHUNK)
        acc = _dot(h[r], w_ref[:, cols])
        if c in q_chunks:
            acc = acc * (HEAD_DIM ** -0.5 * LOG2E)
        if c >= gate_start:
            acc = 1.0 / (1.0 + jnp.exp(-acc))
        o_ref[rows[r], cols] = acc.astype(o_ref.dtype)

    norm(0)
    chunk(0, 0)
    norm(1)
    for c in range(1, n_chunks):
        chunk(1, c - 1)
        chunk(0, c)
    chunk(1, n_chunks - 1)


def _in_proj(x2d, g, w):
    n, d = x2d.shape
    cols = w.shape[1]
    q_chunks = (0, 3 * MIX_WIDTH // COL_CHUNK)
    gate_start = 6 * MIX_WIDTH // COL_CHUNK
    return pl.pallas_call(
        functools.partial(_in_proj_kernel, q_chunks=q_chunks, gate_start=gate_start),
        out_shape=jax.ShapeDtypeStruct((n, cols), BF16),
        grid=(n // ROW_TILE,),
        in_specs=[
            pl.BlockSpec((ROW_TILE, d), lambda i: (i, 0)),
            _resident((1, d)),
            _resident((d, cols)),
        ],
        out_specs=pl.BlockSpec((ROW_TILE, cols), lambda i: (i, 0)),
        compiler_params=_params(1),
        name="in_proj",
    )(x2d, g, w)


MOBA_SKEW = 1


def _moba_kernel(q_ref, k_ref, v_ref, kaux_ref, qaux_ref, o_ref,
                 kaug_ref, vaug_t_ref, km_ref):
    seq = q_ref.shape[1]
    nb = seq // ATT_BLOCK
    lane = lax.broadcasted_iota(jnp.int32, (1, LANES), 1)
    key_t = lax.broadcasted_iota(jnp.int32, (ATT_BLOCK, ATT_BLOCK), 0)
    qry_t = lax.broadcasted_iota(jnp.int32, (ATT_BLOCK, ATT_BLOCK), 1)
    causal = key_t <= qry_t
    lane_q = lax.broadcasted_iota(jnp.int32, (ATT_BLOCK, LANES), 1)
    chan_t = lax.broadcasted_iota(jnp.int32, (LANES, ATT_BLOCK), 0)
    nb_rows = km_ref.shape[1]
    blk_row = lax.broadcasted_iota(jnp.int32, (SUBLANES, ATT_BLOCK), 0)

    k_all = k_ref[0]
    v_all = v_ref[0]
    means = [jnp.mean(k_all[j * ATT_BLOCK:(j + 1) * ATT_BLOCK].astype(F32), axis=0, keepdims=True)
             for j in range(nb)]
    means.append(jnp.zeros((nb_rows - nb, LANES), F32))
    means = jnp.concatenate(means, axis=0)
    v_t = v_all.astype(F32).T
    chan_all = lax.broadcasted_iota(jnp.int32, (LANES, seq), 0)
    for hh in range(HEADS_PER_TILE):
        own = (lane >= hh * HEAD_DIM) & (lane < (hh + 1) * HEAD_DIM)
        own_t = (chan_all >= hh * HEAD_DIM) & (chan_all < (hh + 1) * HEAD_DIM)
        kaug_ref[hh] = jnp.where(own, k_all, kaux_ref[hh])
        vaug_t_ref[hh] = jnp.where(own_t, v_t, 1.0).astype(BF16)
        km_ref[hh] = jnp.where(own, means, 0.0).astype(BF16)

    units = [(i, hh) for i in range(nb - 1, -1, -1) for hh in range(HEADS_PER_TILE)]
    n = len(units)
    q_aug, penalty, tiles, col_max, probs, acc, outs = ({} for _ in range(7))

    def prep(u):
        i, hh = units[u]
        q_i = q_ref[0, i * ATT_BLOCK:(i + 1) * ATT_BLOCK, :]
        own_q = (lane_q >= hh * HEAD_DIM) & (lane_q < (hh + 1) * HEAD_DIM)
        q_aug[u] = jnp.where(own_q, q_i, qaux_ref[0, hh:hh + 1, :].astype(BF16))
        if i > MOBA_TOPK:
            gate = _dot_nt(km_ref[hh], q_i)[:SUBLANES]
            beaten = jnp.zeros((SUBLANES, ATT_BLOCK), F32)
            for jp in range(i):
                g_jp = gate[jp:jp + 1, :]
                wins = (g_jp > gate) | ((g_jp == gate) & (jp < blk_row))
                beaten = beaten + jnp.where(wins, 1.0, 0.0)
            chosen = (blk_row < i) & (beaten < MOBA_TOPK)
            penalty[u] = jnp.where(chosen, 0.0, NEG)

    def qk(u):
        i, hh = units[u]
        ts = []
        for j in range(i + 1):
            s = _dot_nt(kaug_ref[hh, j * ATT_BLOCK:(j + 1) * ATT_BLOCK, :], q_aug[u])
            if j == i:
                s = jnp.where(causal, s, NEG)
            elif i > MOBA_TOPK:
                s = s + penalty[u][j:j + 1, :]
            ts.append(s)
        m_el = ts[0]
        for s in ts[1:]:
            m_el = jnp.maximum(m_el, s)
        tiles[u] = ts
        col_max[u] = jnp.max(m_el, axis=0, keepdims=True)

    def expo(u):
        probs[u] = jnp.concatenate([jnp.exp2(s - col_max[u]).astype(BF16) for s in tiles[u]], axis=0)

    def pv(u):
        i, hh = units[u]
        acc[u] = _dot(vaug_t_ref[hh, :, 0:(i + 1) * ATT_BLOCK], probs[u])

    def fin(u):
        i, hh = units[u]
        spare0 = (1 - hh) * HEAD_DIM
        outs[u] = acc[u] / acc[u][spare0:spare0 + 1, :]
        if hh == HEADS_PER_TILE - 1:
            o_pair_t = jnp.where(chan_t < HEAD_DIM, outs[u - 1], outs[u])
            o_ref[0, i * ATT_BLOCK:(i + 1) * ATT_BLOCK, :] = o_pair_t.T.astype(o_ref.dtype)

    stages = (prep, qk, expo, pv, fin)
    for t in range(n + (len(stages) - 1) * MOBA_SKEW):
        for k, stage in enumerate(stages):
            u = t - k * MOBA_SKEW
            if 0 <= u < n:
                stage(u)


def _bf16_terms(value, n_terms):
    terms, rest = [], float(value)
    for _ in range(n_terms):
        term = float(np.asarray(rest, np.float32).astype(jnp.bfloat16).astype(np.float32))
        terms.append(term)
        rest -= term
    return terms


def _moba_aux(seq):
    pos = np.arange(seq)
    lane = np.arange(LANES)
    pairs = MOBA_HEADS // HEADS_PER_TILE
    kaux = np.zeros((HEADS_PER_TILE, seq, LANES), np.float32)
    qaux = np.zeros((pairs, HEADS_PER_TILE, LANES), np.float32)
    for hh in range(HEADS_PER_TILE):
        j = lane - (1 - hh) * HEAD_DIM
        off_lanes = (j >= 0) & (j < SLOPE_TERMS)
        start_lanes = (j >= SLOPE_TERMS) & (j < 2 * SLOPE_TERMS)
        kaux[hh] = (np.where(off_lanes[None, :], (pos % MOBA_BLOCK)[:, None], 0)
                    + np.where(start_lanes[None, :], ((pos // MOBA_BLOCK) * MOBA_BLOCK)[:, None], 0))
        for p in range(pairs):
            slope = 2.0 ** (-8.0 * (p * HEADS_PER_TILE + hh + 1) / MOBA_HEADS)
            terms = _bf16_terms(slope * LOG2E, SLOPE_TERMS)
            for t, term in enumerate(terms):
                qaux[p, hh, (j == t) | (j == t + SLOPE_TERMS)] = term
    return jnp.asarray(kaux, BF16), jnp.asarray(qaux)


def _moba(proj3, q_col, k_col, v_col):
    b, seq, _ = proj3.shape
    nb = seq // MOBA_BLOCK
    assert seq % MOBA_BLOCK == 0 and nb <= SUBLANES
    pairs = MOBA_HEADS // HEADS_PER_TILE
    kaux, qaux = _moba_aux(seq)
    blk = lambda c0: pl.BlockSpec((1, seq, LANES), lambda bi, p: (bi, 0, c0 + p))
    return pl.pallas_call(
        _moba_kernel,
        out_shape=jax.ShapeDtypeStruct((b, seq, MIX_WIDTH), BF16),
        grid=(b, pairs),
        in_specs=[
            blk(q_col // LANES), blk(k_col // LANES), blk(v_col // LANES),
            _resident((HEADS_PER_TILE, seq, LANES)),
            pl.BlockSpec((1, HEADS_PER_TILE, LANES), lambda bi, p: (p, 0, 0)),
        ],
        out_specs=pl.BlockSpec((1, seq, LANES), lambda bi, p: (bi, 0, p)),
        scratch_shapes=[
            pltpu.VMEM((HEADS_PER_TILE, seq, LANES), BF16),
            pltpu.VMEM((HEADS_PER_TILE, LANES, seq), BF16),
            pltpu.VMEM((HEADS_PER_TILE, 2 * SUBLANES * (-(-nb // (2 * SUBLANES))), LANES), BF16),
        ],
        compiler_params=_params(2),
        name="moba",
    )(proj3, proj3, proj3, kaux, qaux)


SB_DEAD = 200.0
SB_SKEW = 2


def _sb_kernel(q_ref, k_ref, v_ref, tri_ref, o_ref, acc_ref, rest_ref):
    seq = q_ref.shape[1]
    nb = seq // ATT_BLOCK
    row_t = lax.broadcasted_iota(jnp.int32, (ATT_BLOCK, ATT_BLOCK), 0)
    col_t = lax.broadcasted_iota(jnp.int32, (ATT_BLOCK, ATT_BLOCK), 1)
    strict = col_t < row_t
    lane_q = lax.broadcasted_iota(jnp.int32, (ATT_BLOCK, LANES), 1)
    tri = tri_ref[...]
    key_rows = lambda j: slice(j * ATT_BLOCK, (j + 1) * ATT_BLOCK)

    def q_own(i, hh):
        q_i = q_ref[0, i * ATT_BLOCK:(i + 1) * ATT_BLOCK, :]
        own_q = (lane_q >= hh * HEAD_DIM) & (lane_q < (hh + 1) * HEAD_DIM)
        return jnp.where(own_q, q_i, jnp.zeros_like(q_i))

    def run_chains(chains, rest_of):
        n = len(chains)
        z, sp_bf, suffix, a = {}, {}, {}, {}
        outs, new_rests = [None] * n, [None] * n

        def scores(c):
            i, hh, j = chains[c]
            z[c] = _dot_nt(q_own(i, hh), k_ref[0, key_rows(j), :])

        def softplus(c):
            i, hh, j = chains[c]
            neg_abs = lax.bitcast_convert_type(
                lax.bitcast_convert_type(z[c], jnp.uint32) | jnp.uint32(0x80000000), F32)
            sp = jnp.maximum(z[c], 0.0) + jnp.log(1.0 + jnp.exp2(neg_abs)) * LOG2E
            if j == i:
                sp = jnp.where(strict, sp, 0.0)
            sp_bf[c] = sp.astype(BF16)

        def suffix_sum(c):
            suffix[c] = _dot(sp_bf[c], tri)

        def weights(c):
            i, hh, j = chains[c]
            rest = rest_of(c, new_rests)
            w = jnp.exp2(z[c] - suffix[c] - rest)
            if j == i:
                w = jnp.where(strict, w, 0.0)
            a[c] = w.astype(BF16)
            new_rests[c] = rest + suffix[c][:, 0:1]

        def values(c):
            i, hh, j = chains[c]
            outs[c] = _dot(a[c], v_ref[0, key_rows(j), :])

        stages = (scores, softplus, suffix_sum, weights, values)
        for t in range(n + (len(stages) - 1) * SB_SKEW):
            for k, stage in enumerate(stages):
                c = t - k * SB_SKEW
                if 0 <= c < n:
                    stage(c)
        return outs, new_rests

    slot = lambda i, hh: i * HEADS_PER_TILE + hh
    units = lambda d: [(i, hh, i - d) for i in range(d, nb) for hh in range(HEADS_PER_TILE)]

    near = units(0) + (units(1) if nb > 1 else [])
    where = {unit: c for c, unit in enumerate(near)}
    zero_rest = jnp.zeros((ATT_BLOCK, 1), F32)

    def near_rest(c, done):
        i, hh, j = near[c]
        return zero_rest if j == i else done[where[(i, hh, i)]]

    outs, rests = run_chains(near, near_rest)
    for (i, hh, j), o, r in zip(near, outs, rests):
        if j == i and (i, hh, i - 1) in where:
            continue
        total = o if j == i else o + outs[where[(i, hh, i)]]
        acc_ref[slot(i, hh)] = total
        rest_ref[slot(i, hh)] = jnp.broadcast_to(r, (ATT_BLOCK, LANES))

    def alive(d):
        low = rest_ref[slot(d, 0)]
        for s in range(slot(d, 0) + 1, nb * HEADS_PER_TILE):
            low = jnp.minimum(low, rest_ref[s])
        return jnp.min(low) < SB_DEAD

    def far_diagonal(d):
        far = units(d)
        outs, rests = run_chains(far, lambda c, _: rest_ref[slot(far[c][0], far[c][1])][:, 0:1])
        for (i, hh, _), o, r in zip(far, outs, rests):
            acc_ref[slot(i, hh)] = acc_ref[slot(i, hh)] + o
            rest_ref[slot(i, hh)] = jnp.broadcast_to(r, (ATT_BLOCK, LANES))

    def from_diagonal(d):
        far_diagonal(d)
        if d + 1 < nb:
            pl.when(alive(d + 1))(functools.partial(from_diagonal, d + 1))

    if nb > 2:
        pl.when(alive(2))(functools.partial(from_diagonal, 2))

    for i in range(nb):
        o_pair = jnp.where(lane_q < HEAD_DIM, acc_ref[slot(i, 0)], acc_ref[slot(i, 1)])
        o_ref[0, i * ATT_BLOCK:(i + 1) * ATT_BLOCK, :] = o_pair.astype(o_ref.dtype)


def _stick_breaking(proj3, q_col, k_col, v_col):
    b, seq, _ = proj3.shape
    assert seq % ATT_BLOCK == 0
    nb = seq // ATT_BLOCK
    pairs = SB_HEADS // HEADS_PER_TILE
    idx = jnp.arange(ATT_BLOCK)
    tri = (idx[:, None] >= idx[None, :]).astype(BF16)
    blk = lambda c0: pl.BlockSpec((1, seq, LANES), lambda bi, p: (bi, 0, c0 + p))
    state = pltpu.VMEM((nb * HEADS_PER_TILE, ATT_BLOCK, LANES), F32)
    return pl.pallas_call(
        _sb_kernel,
        out_shape=jax.ShapeDtypeStruct((b, seq, MIX_WIDTH), BF16),
        grid=(b, pairs),
        in_specs=[
            blk(q_col // LANES), blk(k_col // LANES), blk(v_col // LANES),
            _resident(tri.shape),
        ],
        out_specs=pl.BlockSpec((1, seq, LANES), lambda bi, p: (bi, 0, p)),
        scratch_shapes=[state, state],
        compiler_params=_params(2),
        name="stick_breaking",
    )(proj3, proj3, proj3, tri)


def _mem_kv_kernel(mem_ref, g_ref, w_ref, o_ref):
    h = _rms_scale(mem_ref[0], g_ref[...]).astype(BF16)
    o_ref[0] = _dot(h, w_ref[...]).astype(o_ref.dtype)


def _mem_kv(mem, g, w):
    b, m, d = mem.shape
    return pl.pallas_call(
        _mem_kv_kernel,
        out_shape=jax.ShapeDtypeStruct((b, m, w.shape[1]), BF16),
        grid=(b,),
        in_specs=[pl.BlockSpec((1, m, d), lambda i: (i, 0, 0)), _resident((1, d)), _resident(w.shape)],
        out_specs=pl.BlockSpec((1, m, w.shape[1]), lambda i: (i, 0, 0)),
        compiler_params=_params(1),
        name="mem_kv",
    )(mem, g, w)


def _mix_cross_kernel(a_ref, b_ref, ga_ref, gb_ref, x_ref, kv_ref, g_ref,
                      wa_ref, wb_ref, wo_ref, wq_ref, wxo_ref, o_ref):
    half = x_ref.shape[1] // 2
    rows = (slice(0, half), slice(half, 2 * half))
    ya, yb, merged, x1, q, probs, denom, attn = ({} for _ in range(8))
    head_cols = lambda hd: slice(hd * X_HEAD_DIM, (hd + 1) * X_HEAD_DIM)

    def branches(r):
        ya[r] = _dot(a_ref[0, rows[r], :], wa_ref[...])
        yb[r] = _dot(b_ref[0, rows[r], :], wb_ref[...])

    def merge(r):
        m = ga_ref[0, rows[r], :].astype(F32) * ya[r] + gb_ref[0, rows[r], :].astype(F32) * yb[r]
        merged[r] = m.astype(BF16)

    def mix_residual(r):
        x1[r] = x_ref[0, rows[r], :] + _dot(merged[r], wo_ref[...])

    def project(r):
        h = _rms_scale(x1[r], g_ref[...]).astype(BF16)
        q[r] = (_dot(h, wq_ref[...]) * (X_HEAD_DIM ** -0.5 * LOG2E)).astype(BF16)

    def scores(r):
        probs[r], denom[r] = [], []
        for hd in range(X_HEADS):
            s = _dot_nt(q[r][:, head_cols(hd)], kv_ref[0, :, head_cols(hd)])
            p = jnp.exp2(s - jnp.max(s, axis=1, keepdims=True))
            denom[r].append(jnp.sum(p, axis=1, keepdims=True))
            probs[r].append(p.astype(BF16))

    def values(r):
        heads = []
        for hd in range(X_HEADS):
            v_h = kv_ref[0, :, X_WIDTH + hd * X_HEAD_DIM:X_WIDTH + (hd + 1) * X_HEAD_DIM]
            heads.append((_dot(probs[r][hd], v_h) / denom[r][hd]).astype(BF16))
        attn[r] = jnp.concatenate(heads, axis=1)

    def cross_residual(r):
        o_ref[0, rows[r], :] = x1[r] + _dot(attn[r], wxo_ref[...])

    stages = (branches, merge, mix_residual, project, scores, values, cross_residual)
    for t in range(len(rows) + len(stages) - 1):
        for k, stage in enumerate(stages):
            if 0 <= t - k < len(rows):
                stage(t - k)


def _mix_cross(moba_o, sb_o, proj3, gate_col, x3, kv, g, wa, wb, wo, wq, wxo):
    b, seq, d = x3.shape
    m = kv.shape[1]
    rows = lambda w, c=0: pl.BlockSpec((1, ROW_TILE, w), lambda bi, i: (bi, i, c))
    return pl.pallas_call(
        _mix_cross_kernel,
        out_shape=jax.ShapeDtypeStruct((b, seq, d), F32),
        grid=(b, seq // ROW_TILE),
        in_specs=[
            rows(MIX_WIDTH), rows(MIX_WIDTH),
            rows(d, gate_col // d), rows(d, gate_col // d + 1),
            rows(d),
            pl.BlockSpec((1, m, kv.shape[2]), lambda bi, i: (bi, 0, 0)),
            _resident((1, d)),
            _resident(wa.shape), _resident(wb.shape), _resident(wo.shape),
            _resident(wq.shape), _resident(wxo.shape),
        ],
        out_specs=rows(d),
        compiler_params=_params(2),
        name="mix_cross",
    )(moba_o, sb_o, proj3, proj3, x3, kv, g, wa, wb, wo, wq, wxo)


def _mlp_kernel(x_ref, g_ref, wu_ref, wd_ref, gf_ref, o_ref, *, final_norm):
    x = x_ref[...]
    h = _rms_scale(x, g_ref[...]).astype(BF16)
    acc = x
    for c in range(wu_ref.shape[1] // COL_CHUNK):
        cols = slice(c * COL_CHUNK, (c + 1) * COL_CHUNK)
        u = jnp.maximum(_dot(h, wu_ref[:, cols]), 0.0)
        acc = acc + _dot((u * u).astype(BF16), wd_ref[cols, :])
    o_ref[...] = _rms_scale(acc, gf_ref[...]) if final_norm else acc


def _mlp(x2d, g, wu, wd, gf, final_norm):
    n, d = x2d.shape
    return pl.pallas_call(
        functools.partial(_mlp_kernel, final_norm=final_norm),
        out_shape=jax.ShapeDtypeStruct((n, d), F32),
        grid=(n // ROW_TILE,),
        in_specs=[
            pl.BlockSpec((ROW_TILE, d), lambda i: (i, 0)),
            _resident((1, d)), _resident(wu.shape), _resident(wd.shape), _resident((1, d)),
        ],
        out_specs=pl.BlockSpec((ROW_TILE, d), lambda i: (i, 0)),
        compiler_params=_params(1),
        name="mlp_final",
    )(x2d, g, wu, wd, gf)


def kernel(x, mem, g_mix, w_in, w_br_moba, w_br_sb, w_out, g_cross, g_mem,
           w_xq, w_xkv, w_xo, g_mlp, w_up, w_down, g_final):
    b, seq, d = x.shape
    n = b * seq
    assert n % ROW_TILE == 0 and seq % ROW_TILE == 0
    row = lambda v: v.reshape(1, -1).astype(F32)
    depth = g_mix.shape[0]
    x2d = x.reshape(n, d)
    for l in range(depth):
        proj = _in_proj(x2d, row(g_mix[l]), w_in[l].astype(BF16))
        proj3 = proj.reshape(b, seq, -1)
        moba_o = _moba(proj3, 0, MIX_WIDTH, 2 * MIX_WIDTH)
        sb_o = _stick_breaking(proj3, 3 * MIX_WIDTH, 4 * MIX_WIDTH, 5 * MIX_WIDTH)
        kv = _mem_kv(mem, row(g_mem[l]), w_xkv[l].astype(BF16))
        x3 = _mix_cross(moba_o, sb_o, proj3, 6 * MIX_WIDTH, x2d.reshape(b, seq, d), kv, row(g_cross[l]),
                        w_br_moba[l].astype(BF16), w_br_sb[l].astype(BF16), w_out[l].astype(BF16),
                        w_xq[l].astype(BF16), w_xo[l].astype(BF16))
        x2d = _mlp(x3.reshape(n, d), row(g_mlp[l]), w_up[l].astype(BF16), w_down[l].astype(BF16),
                   row(g_final), final_norm=(l == depth - 1))
    return x2d.reshape(b, seq, d)
```

```python
import functools
import math

import jax
import jax.numpy as jnp
import numpy as np
from jax import lax
from jax.experimental import pallas as pl
from jax.experimental.pallas import tpu as pltpu

F32 = jnp.float32
BF16 = jnp.bfloat16

HEAD_DIM = 64
MOBA_HEADS = 8
SB_HEADS = 8
MIX_WIDTH = MOBA_HEADS * HEAD_DIM
MOBA_BLOCK = 256
MOBA_TOPK = 3
X_HEADS = 4
X_HEAD_DIM = 128
X_WIDTH = X_HEADS * X_HEAD_DIM
RMS_EPS = 1e-6
NEG = -1e30
LOG2E = math.log2(math.e)
SLOPE_TERMS = 4

LANES = 128
SUBLANES = 8
HEADS_PER_TILE = LANES // HEAD_DIM
ATT_BLOCK = 256
ROW_TILE = 512
WIDE_ROW_TILE = 1024
COL_CHUNK = 512
VMEM_LIMIT = 56 * 1024 * 1024


def _params(n_axes):
    return pltpu.CompilerParams(
        dimension_semantics=("arbitrary",) * n_axes, vmem_limit_bytes=VMEM_LIMIT)


def _resident(shape):
    return pl.BlockSpec(shape, lambda *_: (0,) * len(shape), pipeline_mode=pl.Buffered(1))


def _rms_scale(x, g):
    y = x * lax.rsqrt(jnp.mean(x * x, axis=-1, keepdims=True) + RMS_EPS)
    return y * g


def _dot(a, b):
    return jnp.dot(a, b, preferred_element_type=F32)


def _dot_nt(a, b):
    return lax.dot_general(a, b, (((1,), (1,)), ((), ())), preferred_element_type=F32)


def _in_proj_kernel(x_ref, g_ref, w_ref, o_ref, *, q_chunks, gate_start):
    n_chunks = w_ref.shape[1] // COL_CHUNK
    half = x_ref.shape[0] // 2
    rows = (slice(0, half), slice(half, 2 * half))
    h = {}

    def norm(r):
        h[r] = _rms_scale(x_ref[rows[r], :], g_ref[...]).astype(BF16)

    def chunk(r, c):
        cols = slice(c * COL_CHUNK, (c + 1) * COL_CHUNK)
        acc = _dot(h[r], w_ref[:, cols])
        if c in q_chunks:
            acc = acc * (HEAD_DIM ** -0.5 * LOG2E)
        if c >= gate_start:
            acc = 1.0 / (1.0 + jnp.exp(-acc))
        o_ref[rows[r], cols] = acc.astype(o_ref.dtype)

    norm(0)
    chunk(0, 0)
    norm(1)
    for c in range(1, n_chunks):
        chunk(1, c - 1)
        chunk(0, c)
    chunk(1, n_chunks - 1)


def _in_proj(x2d, g, w):
    n, d = x2d.shape
    cols = w.shape[1]
    q_chunks = (0, 3 * MIX_WIDTH // COL_CHUNK)
    gate_start = 6 * MIX_WIDTH // COL_CHUNK
    return pl.pallas_call(
        functools.partial(_in_proj_kernel, q_chunks=q_chunks, gate_start=gate_start),
        out_shape=jax.ShapeDtypeStruct((n, cols), BF16),
        grid=(n // ROW_TILE,),
        in_specs=[
            pl.BlockSpec((ROW_TILE, d), lambda i: (i, 0)),
            _resident((1, d)),
            _resident((d, cols)),
        ],
        out_specs=pl.BlockSpec((ROW_TILE, cols), lambda i: (i, 0)),
        compiler_params=_params(1),
        name="in_proj",
    )(x2d, g, w)


MOBA_SKEW = 1


def _moba_kernel(q_ref, k_ref, v_ref, kaux_ref, qaux_ref, o_ref,
                 kaug_ref, vaug_t_ref, km_ref):
    seq = q_ref.shape[1]
    nb = seq // ATT_BLOCK
    lane = lax.broadcasted_iota(jnp.int32, (1, LANES), 1)
    key_t = lax.broadcasted_iota(jnp.int32, (ATT_BLOCK, ATT_BLOCK), 0)
    qry_t = lax.broadcasted_iota(jnp.int32, (ATT_BLOCK, ATT_BLOCK), 1)
    causal = key_t <= qry_t
    lane_q = lax.broadcasted_iota(jnp.int32, (ATT_BLOCK, LANES), 1)
    chan_t = lax.broadcasted_iota(jnp.int32, (LANES, ATT_BLOCK), 0)
    nb_rows = km_ref.shape[1]
    blk_row = lax.broadcasted_iota(jnp.int32, (SUBLANES, ATT_BLOCK), 0)

    k_all = k_ref[0]
    v_all = v_ref[0]
    means = [jnp.mean(k_all[j * ATT_BLOCK:(j + 1) * ATT_BLOCK].astype(F32), axis=0, keepdims=True)
             for j in range(nb)]
    means.append(jnp.zeros((nb_rows - nb, LANES), F32))
    means = jnp.concatenate(means, axis=0)
    v_t = v_all.astype(F32).T
    chan_all = lax.broadcasted_iota(jnp.int32, (LANES, seq), 0)
    for hh in range(HEADS_PER_TILE):
        own = (lane >= hh * HEAD_DIM) & (lane < (hh + 1) * HEAD_DIM)
        own_t = (chan_all >= hh * HEAD_DIM) & (chan_all < (hh + 1) * HEAD_DIM)
        kaug_ref[hh] = jnp.where(own, k_all, kaux_ref[hh])
        vaug_t_ref[hh] = jnp.where(own_t, v_t, 1.0).astype(BF16)
        km_ref[hh] = jnp.where(own, means, 0.0).astype(BF16)

    units = [(i, hh) for i in range(nb - 1, -1, -1) for hh in range(HEADS_PER_TILE)]
    n = len(units)
    q_aug, penalty, tiles, col_max, probs, acc, outs = ({} for _ in range(7))

    def prep(u):
        i, hh = units[u]
        q_i = q_ref[0, i * ATT_BLOCK:(i + 1) * ATT_BLOCK, :]
        own_q = (lane_q >= hh * HEAD_DIM) & (lane_q < (hh + 1) * HEAD_DIM)
        q_aug[u] = jnp.where(own_q, q_i, qaux_ref[0, hh:hh + 1, :].astype(BF16))
        if i > MOBA_TOPK:
            gate = _dot_nt(km_ref[hh], q_i)[:SUBLANES]
            beaten = jnp.zeros((SUBLANES, ATT_BLOCK), F32)
            for jp in range(i):
                g_jp = gate[jp:jp + 1, :]
                wins = (g_jp > gate) | ((g_jp == gate) & (jp < blk_row))
                beaten = beaten + jnp.where(wins, 1.0, 0.0)
            chosen = (blk_row < i) & (beaten < MOBA_TOPK)
            penalty[u] = jnp.where(chosen, 0.0, NEG)

    def qk(u):
        i, hh = units[u]
        ts = []
        for j in range(i + 1):
            s = _dot_nt(kaug_ref[hh, j * ATT_BLOCK:(j + 1) * ATT_BLOCK, :], q_aug[u])
            if j == i:
                s = jnp.where(causal, s, NEG)
            elif i > MOBA_TOPK:
                s = s + penalty[u][j:j + 1, :]
            ts.append(s)
        m_el = ts[0]
        for s in ts[1:]:
            m_el = jnp.maximum(m_el, s)
        tiles[u] = ts
        col_max[u] = jnp.max(m_el, axis=0, keepdims=True)

    def expo(u):
        probs[u] = jnp.concatenate([jnp.exp2(s - col_max[u]).astype(BF16) for s in tiles[u]], axis=0)

    def pv(u):
        i, hh = units[u]
        acc[u] = _dot(vaug_t_ref[hh, :, 0:(i + 1) * ATT_BLOCK], probs[u])

    def fin(u):
        i, hh = units[u]
        spare0 = (1 - hh) * HEAD_DIM
        outs[u] = acc[u] / acc[u][spare0:spare0 + 1, :]
        if hh == HEADS_PER_TILE - 1:
            o_pair_t = jnp.where(chan_t < HEAD_DIM, outs[u - 1], outs[u])
            o_ref[0, i * ATT_BLOCK:(i + 1) * ATT_BLOCK, :] = o_pair_t.T.astype(o_ref.dtype)

    stages = (prep, qk, expo, pv, fin)
    for t in range(n + (len(stages) - 1) * MOBA_SKEW):
        for k, stage in enumerate(stages):
            u = t - k * MOBA_SKEW
            if 0 <= u < n:
                stage(u)


def _bf16_terms(value, n_terms):
    terms, rest = [], float(value)
    for _ in range(n_terms):
        term = float(np.asarray(rest, np.float32).astype(jnp.bfloat16).astype(np.float32))
        terms.append(term)
        rest -= term
    return terms


def _moba_aux(seq):
    pos = np.arange(seq)
    lane = np.arange(LANES)
    pairs = MOBA_HEADS // HEADS_PER_TILE
    kaux = np.zeros((HEADS_PER_TILE, seq, LANES), np.float32)
    qaux = np.zeros((pairs, HEADS_PER_TILE, LANES), np.float32)
    for hh in range(HEADS_PER_TILE):
        j = lane - (1 - hh) * HEAD_DIM
        off_lanes = (j >= 0) & (j < SLOPE_TERMS)
        start_lanes = (j >= SLOPE_TERMS) & (j < 2 * SLOPE_TERMS)
        kaux[hh] = (np.where(off_lanes[None, :], (pos % MOBA_BLOCK)[:, None], 0)
                    + np.where(start_lanes[None, :], ((pos // MOBA_BLOCK) * MOBA_BLOCK)[:, None], 0))
        for p in range(pairs):
            slope = 2.0 ** (-8.0 * (p * HEADS_PER_TILE + hh + 1) / MOBA_HEADS)
            terms = _bf16_terms(slope * LOG2E, SLOPE_TERMS)
            for t, term in enumerate(terms):
                qaux[p, hh, (j == t) | (j == t + SLOPE_TERMS)] = term
    return jnp.asarray(kaux, BF16), jnp.asarray(qaux)


def _moba(proj3, q_col, k_col, v_col):
    b, seq, _ = proj3.shape
    nb = seq // MOBA_BLOCK
    assert seq % MOBA_BLOCK == 0 and nb <= SUBLANES
    pairs = MOBA_HEADS // HEADS_PER_TILE
    kaux, qaux = _moba_aux(seq)
    blk = lambda c0: pl.BlockSpec((1, seq, LANES), lambda bi, p: (bi, 0, c0 + p))
    return pl.pallas_call(
        _moba_kernel,
        out_shape=jax.ShapeDtypeStruct((b, seq, MIX_WIDTH), BF16),
        grid=(b, pairs),
        in_specs=[
            blk(q_col // LANES), blk(k_col // LANES), blk(v_col // LANES),
            _resident((HEADS_PER_TILE, seq, LANES)),
            pl.BlockSpec((1, HEADS_PER_TILE, LANES), lambda bi, p: (p, 0, 0)),
        ],
        out_specs=pl.BlockSpec((1, seq, LANES), lambda bi, p: (bi, 0, p)),
        scratch_shapes=[
            pltpu.VMEM((HEADS_PER_TILE, seq, LANES), BF16),
            pltpu.VMEM((HEADS_PER_TILE, LANES, seq), BF16),
            pltpu.VMEM((HEADS_PER_TILE, 2 * SUBLANES * (-(-nb // (2 * SUBLANES))), LANES), BF16),
        ],
        compiler_params=_params(2),
        name="moba",
    )(proj3, proj3, proj3, kaux, qaux)


SB_DEAD = 200.0
SB_SKEW = 2


def _sb_kernel(q_ref, k_ref, v_ref, tri_ref, o_ref, acc_ref, rest_ref):
    seq = q_ref.shape[1]
    nb = seq // ATT_BLOCK
    row_t = lax.broadcasted_iota(jnp.int32, (ATT_BLOCK, ATT_BLOCK), 0)
    col_t = lax.broadcasted_iota(jnp.int32, (ATT_BLOCK, ATT_BLOCK), 1)
    strict = col_t < row_t
    lane_q = lax.broadcasted_iota(jnp.int32, (ATT_BLOCK, LANES), 1)
    tri = tri_ref[...]
    key_rows = lambda j: slice(j * ATT_BLOCK, (j + 1) * ATT_BLOCK)

    def q_own(i, hh):
        q_i = q_ref[0, i * ATT_BLOCK:(i + 1) * ATT_BLOCK, :]
        own_q = (lane_q >= hh * HEAD_DIM) & (lane_q < (hh + 1) * HEAD_DIM)
        return jnp.where(own_q, q_i, jnp.zeros_like(q_i))

    def run_chains(chains, rest_of):
        n = len(chains)
        z, sp_bf, suffix, a = {}, {}, {}, {}
        outs, new_rests = [None] * n, [None] * n

        def scores(c):
            i, hh, j = chains[c]
            z[c] = _dot_nt(q_own(i, hh), k_ref[0, key_rows(j), :])

        def softplus(c):
            i, hh, j = chains[c]
            neg_abs = lax.bitcast_convert_type(
                lax.bitcast_convert_type(z[c], jnp.uint32) | jnp.uint32(0x80000000), F32)
            sp = jnp.maximum(z[c], 0.0) + jnp.log(1.0 + jnp.exp2(neg_abs)) * LOG2E
            if j == i:
                sp = jnp.where(strict, sp, 0.0)
            sp_bf[c] = sp.astype(BF16)

        def suffix_sum(c):
            suffix[c] = _dot(sp_bf[c], tri)

        def weights(c):
            i, hh, j = chains[c]
            rest = rest_of(c, new_rests)
            w = jnp.exp2(z[c] - suffix[c] - rest)
            if j == i:
                w = jnp.where(strict, w, 0.0)
            a[c] = w.astype(BF16)
            new_rests[c] = rest + suffix[c][:, 0:1]

        def values(c):
            i, hh, j = chains[c]
            outs[c] = _dot(a[c], v_ref[0, key_rows(j), :])

        stages = (scores, softplus, suffix_sum, weights, values)
        for t in range(n + (len(stages) - 1) * SB_SKEW):
            for k, stage in enumerate(stages):
                c = t - k * SB_SKEW
                if 0 <= c < n:
                    stage(c)
        return outs, new_rests

    slot = lambda i, hh: i * HEADS_PER_TILE + hh
    units = lambda d: [(i, hh, i - d) for i in range(d, nb) for hh in range(HEADS_PER_TILE)]

    near = units(0) + (units(1) if nb > 1 else [])
    where = {unit: c for c, unit in enumerate(near)}
    zero_rest = jnp.zeros((ATT_BLOCK, 1), F32)

    def near_rest(c, done):
        i, hh, j = near[c]
        return zero_rest if j == i else done[where[(i, hh, i)]]

    outs, rests = run_chains(near, near_rest)
    for (i, hh, j), o, r in zip(near, outs, rests):
        if j == i and (i, hh, i - 1) in where:
            continue
        total = o if j == i else o + outs[where[(i, hh, i)]]
        acc_ref[slot(i, hh)] = total
        rest_ref[slot(i, hh)] = jnp.broadcast_to(r, (ATT_BLOCK, LANES))

    def alive(d):
        low = rest_ref[slot(d, 0)]
        for s in range(slot(d, 0) + 1, nb * HEADS_PER_TILE):
            low = jnp.minimum(low, rest_ref[s])
        return jnp.min(low) < SB_DEAD

    def far_diagonal(d):
        far = units(d)
        outs, rests = run_chains(far, lambda c, _: rest_ref[slot(far[c][0], far[c][1])][:, 0:1])
        for (i, hh, _), o, r in zip(far, outs, rests):
            acc_ref[slot(i, hh)] = acc_ref[slot(i, hh)] + o
            rest_ref[slot(i, hh)] = jnp.broadcast_to(r, (ATT_BLOCK, LANES))

    def from_diagonal(d):
        far_diagonal(d)
        if d + 1 < nb:
            pl.when(alive(d + 1))(functools.partial(from_diagonal, d + 1))

    if nb > 2:
        pl.when(alive(2))(functools.partial(from_diagonal, 2))

    for i in range(nb):
        o_pair = jnp.where(lane_q < HEAD_DIM, acc_ref[slot(i, 0)], acc_ref[slot(i, 1)])
        o_ref[0, i * ATT_BLOCK:(i + 1) * ATT_BLOCK, :] = o_pair.astype(o_ref.dtype)


def _stick_breaking(proj3, q_col, k_col, v_col):
    b, seq, _ = proj3.shape
    assert seq % ATT_BLOCK == 0
    nb = seq // ATT_BLOCK
    pairs = SB_HEADS // HEADS_PER_TILE
    idx = jnp.arange(ATT_BLOCK)
    tri = (idx[:, None] >= idx[None, :]).astype(BF16)
    blk = lambda c0: pl.BlockSpec((1, seq, LANES), lambda bi, p: (bi, 0, c0 + p))
    state = pltpu.VMEM((nb * HEADS_PER_TILE, ATT_BLOCK, LANES), F32)
    return pl.pallas_call(
        _sb_kernel,
        out_shape=jax.ShapeDtypeStruct((b, seq, MIX_WIDTH), BF16),
        grid=(b, pairs),
        in_specs=[
            blk(q_col // LANES), blk(k_col // LANES), blk(v_col // LANES),
            _resident(tri.shape),
        ],
        out_specs=pl.BlockSpec((1, seq, LANES), lambda bi, p: (bi, 0, p)),
        scratch_shapes=[state, state],
        compiler_params=_params(2),
        name="stick_breaking",
    )(proj3, proj3, proj3, tri)


def _mem_kv_kernel(mem_ref, g_ref, w_ref, o_ref):
    h = _rms_scale(mem_ref[0], g_ref[...]).astype(BF16)
    o_ref[0] = _dot(h, w_ref[...]).astype(o_ref.dtype)


def _mem_kv(mem, g, w):
    b, m, d = mem.shape
    return pl.pallas_call(
        _mem_kv_kernel,
        out_shape=jax.ShapeDtypeStruct((b, m, w.shape[1]), BF16),
        grid=(b,),
        in_specs=[pl.BlockSpec((1, m, d), lambda i: (i, 0, 0)), _resident((1, d)), _resident(w.shape)],
        out_specs=pl.BlockSpec((1, m, w.shape[1]), lambda i: (i, 0, 0)),
        compiler_params=_params(1),
        name="mem_kv",
    )(mem, g, w)


def _mix_cross_kernel(a_ref, b_ref, ga_ref, gb_ref, x_ref, kv_ref, g_ref,
                      wa_ref, wb_ref, wo_ref, wq_ref, wxo_ref, o_ref):
    half = x_ref.shape[1] // 2
    rows = (slice(0, half), slice(half, 2 * half))
    ya, yb, merged, x1, q, probs, denom, attn = ({} for _ in range(8))
    head_cols = lambda hd: slice(hd * X_HEAD_DIM, (hd + 1) * X_HEAD_DIM)

    def branches(r):
        ya[r] = _dot(a_ref[0, rows[r], :], wa_ref[...])
        yb[r] = _dot(b_ref[0, rows[r], :], wb_ref[...])

    def merge(r):
        m = ga_ref[0, rows[r], :].astype(F32) * ya[r] + gb_ref[0, rows[r], :].astype(F32) * yb[r]
        merged[r] = m.astype(BF16)

    def mix_residual(r):
        x1[r] = x_ref[0, rows[r], :] + _dot(merged[r], wo_ref[...])

    def project(r):
        h = _rms_scale(x1[r], g_ref[...]).astype(BF16)
        q[r] = (_dot(h, wq_ref[...]) * (X_HEAD_DIM ** -0.5 * LOG2E)).astype(BF16)

    def scores(r):
        probs[r], denom[r] = [], []
        for hd in range(X_HEADS):
            s = _dot_nt(q[r][:, head_cols(hd)], kv_ref[0, :, head_cols(hd)])
            p = jnp.exp2(s - jnp.max(s, axis=1, keepdims=True))
            denom[r].append(jnp.sum(p, axis=1, keepdims=True))
            probs[r].append(p.astype(BF16))

    def values(r):
        heads = []
        for hd in range(X_HEADS):
            v_h = kv_ref[0, :, X_WIDTH + hd * X_HEAD_DIM:X_WIDTH + (hd + 1) * X_HEAD_DIM]
            heads.append((_dot(probs[r][hd], v_h) / denom[r][hd]).astype(BF16))
        attn[r] = jnp.concatenate(heads, axis=1)

    def cross_residual(r):
        o_ref[0, rows[r], :] = x1[r] + _dot(attn[r], wxo_ref[...])

    stages = (branches, merge, mix_residual, project, scores, values, cross_residual)
    for t in range(len(rows) + len(stages) - 1):
        for k, stage in enumerate(stages):
            if 0 <= t - k < len(rows):
                stage(t - k)


def _mix_cross(moba_o, sb_o, proj3, gate_col, x3, kv, g, wa, wb, wo, wq, wxo):
    b, seq, d = x3.shape
    m = kv.shape[1]
    rows = lambda w, c=0: pl.BlockSpec((1, WIDE_ROW_TILE, w), lambda bi, i: (bi, i, c))
    return pl.pallas_call(
        _mix_cross_kernel,
        out_shape=jax.ShapeDtypeStruct((b, seq, d), F32),
        grid=(b, seq // WIDE_ROW_TILE),
        in_specs=[
            rows(MIX_WIDTH), rows(MIX_WIDTH),
            rows(d, gate_col // d), rows(d, gate_col // d + 1),
            rows(d),
            pl.BlockSpec((1, m, kv.shape[2]), lambda bi, i: (bi, 0, 0)),
            _resident((1, d)),
            _resident(wa.shape), _resident(wb.shape), _resident(wo.shape),
            _resident(wq.shape), _resident(wxo.shape),
        ],
        out_specs=rows(d),
        compiler_params=_params(2),
        name="mix_cross",
    )(moba_o, sb_o, proj3, proj3, x3, kv, g, wa, wb, wo, wq, wxo)


def _mlp_kernel(x_ref, g_ref, wu_ref, wd_ref, gf_ref, o_ref, *, final_norm):
    x = x_ref[...]
    h = _rms_scale(x, g_ref[...]).astype(BF16)
    acc = x
    for c in range(wu_ref.shape[1] // COL_CHUNK):
        cols = slice(c * COL_CHUNK, (c + 1) * COL_CHUNK)
        u = jnp.maximum(_dot(h, wu_ref[:, cols]), 0.0)
        acc = acc + _dot((u * u).astype(BF16), wd_ref[cols, :])
    o_ref[...] = _rms_scale(acc, gf_ref[...]) if final_norm else acc


def _mlp(x2d, g, wu, wd, gf, final_norm):
    n, d = x2d.shape
    return pl.pallas_call(
        functools.partial(_mlp_kernel, final_norm=final_norm),
        out_shape=jax.ShapeDtypeStruct((n, d), F32),
        grid=(n // WIDE_ROW_TILE,),
        in_specs=[
            pl.BlockSpec((WIDE_ROW_TILE, d), lambda i: (i, 0)),
            _resident((1, d)), _resident(wu.shape), _resident(wd.shape), _resident((1, d)),
        ],
        out_specs=pl.BlockSpec((WIDE_ROW_TILE, d), lambda i: (i, 0)),
        compiler_params=_params(1),
        name="mlp_final",
    )(x2d, g, wu, wd, gf)


def kernel(x, mem, g_mix, w_in, w_br_moba, w_br_sb, w_out, g_cross, g_mem,
           w_xq, w_xkv, w_xo, g_mlp, w_up, w_down, g_final):
    b, seq, d = x.shape
    n = b * seq
    assert n % ROW_TILE == 0 and seq % WIDE_ROW_TILE == 0
    row = lambda v: v.reshape(1, -1).astype(F32)
    depth = g_mix.shape[0]
    x2d = x.reshape(n, d)
    for l in range(depth):
        proj = _in_proj(x2d, row(g_mix[l]), w_in[l].astype(BF16))
        proj3 = proj.reshape(b, seq, -1)
        moba_o = _moba(proj3, 0, MIX_WIDTH, 2 * MIX_WIDTH)
        sb_o = _stick_breaking(proj3, 3 * MIX_WIDTH, 4 * MIX_WIDTH, 5 * MIX_WIDTH)
        kv = _mem_kv(mem, row(g_mem[l]), w_xkv[l].astype(BF16))
        x3 = _mix_cross(moba_o, sb_o, proj3, 6 * MIX_WIDTH, x2d.reshape(b, seq, d), kv, row(g_cross[l]),
                        w_br_moba[l].astype(BF16), w_br_sb[l].astype(BF16), w_out[l].astype(BF16),
                        w_xq[l].astype(BF16), w_xo[l].astype(BF16))
        x2d = _mlp(x3.reshape(n, d), row(g_mlp[l]), w_up[l].astype(BF16), w_down[l].astype(BF16),
                   row(g_final), final_norm=(l == depth - 1))
    return x2d.reshape(b, seq, d)
```

```python
import functools
import math

import jax
import jax.numpy as jnp
import numpy as np
from jax import lax
from jax.experimental import pallas as pl
from jax.experimental.pallas import tpu as pltpu

F32 = jnp.float32
BF16 = jnp.bfloat16

HEAD_DIM = 64
MOBA_HEADS = 8
SB_HEADS = 8
MIX_WIDTH = MOBA_HEADS * HEAD_DIM
MOBA_BLOCK = 256
MOBA_TOPK = 3
X_HEADS = 4
X_HEAD_DIM = 128
X_WIDTH = X_HEADS * X_HEAD_DIM
RMS_EPS = 1e-6
NEG = -1e30
LOG2E = math.log2(math.e)
SLOPE_TERMS = 4

LANES = 128
SUBLANES = 8
HEADS_PER_TILE = LANES // HEAD_DIM
ATT_BLOCK = 256
ROW_TILE = 512
WIDE_ROW_TILE = 1024
COL_CHUNK = 512
VMEM_LIMIT = 56 * 1024 * 1024


def _params(n_axes):
    return pltpu.CompilerParams(
        dimension_semantics=("arbitrary",) * n_axes, vmem_limit_bytes=VMEM_LIMIT)


def _resident(shape):
    return pl.BlockSpec(shape, lambda *_: (0,) * len(shape), pipeline_mode=pl.Buffered(1))


def _rms_scale(x, g):
    y = x * lax.rsqrt(jnp.mean(x * x, axis=-1, keepdims=True) + RMS_EPS)
    return y * g


def _dot(a, b):
    return jnp.dot(a, b, preferred_element_type=F32)


def _dot_nt(a, b):
    return lax.dot_general(a, b, (((1,), (1,)), ((), ())), preferred_element_type=F32)


def _in_proj_kernel(x_ref, g_ref, w_ref, o_ref, *, q_chunks, gate_start):
    n_chunks = w_ref.shape[1] // COL_CHUNK
    half = x_ref.shape[0] // 2
    rows = (slice(0, half), slice(half, 2 * half))
    h = {}

    def norm(r):
        h[r] = _rms_scale(x_ref[rows[r], :], g_ref[...]).astype(BF16)

    def chunk(r, c):
        cols = slice(c * COL_CHUNK, (c + 1) * COL_CHUNK)
        acc = _dot(h[r], w_ref[:, cols])
        if c in q_chunks:
            acc = acc * (HEAD_DIM ** -0.5 * LOG2E)
        if c >= gate_start:
            acc = 1.0 / (1.0 + jnp.exp(-acc))
        o_ref[rows[r], cols] = acc.astype(o_ref.dtype)

    norm(0)
    chunk(0, 0)
    norm(1)
    for c in range(1, n_chunks):
        chunk(1, c - 1)
        chunk(0, c)
    chunk(1, n_chunks - 1)


def _in_proj(x2d, g, w):
    n, d = x2d.shape
    cols = w.shape[1]
    q_chunks = (0, 3 * MIX_WIDTH // COL_CHUNK)
    gate_start = 6 * MIX_WIDTH // COL_CHUNK
    return pl.pallas_call(
        functools.partial(_in_proj_kernel, q_chunks=q_chunks, gate_start=gate_start),
        out_shape=jax.ShapeDtypeStruct((n, cols), BF16),
        grid=(n // ROW_TILE,),
        in_specs=[
            pl.BlockSpec((ROW_TILE, d), lambda i: (i, 0)),
            _resident((1, d)),
            _resident((d, cols)),
        ],
        out_specs=pl.BlockSpec((ROW_TILE, cols), lambda i: (i, 0)),
        compiler_params=_params(1),
        name="in_proj",
    )(x2d, g, w)


MOBA_SKEW = 1
MOBA_TILES = 2


def _moba_kernel(q_ref, k_ref, v_ref, kaux_ref, qaux_ref, o_ref,
                 kaug_ref, vaug_t_ref, km_ref):
    seq = q_ref.shape[1]
    nb = seq // ATT_BLOCK
    lane = lax.broadcasted_iota(jnp.int32, (1, LANES), 1)
    key_t = lax.broadcasted_iota(jnp.int32, (ATT_BLOCK, ATT_BLOCK), 0)
    qry_t = lax.broadcasted_iota(jnp.int32, (ATT_BLOCK, ATT_BLOCK), 1)
    causal = key_t <= qry_t
    lane_q = lax.broadcasted_iota(jnp.int32, (ATT_BLOCK, LANES), 1)
    chan_t = lax.broadcasted_iota(jnp.int32, (LANES, ATT_BLOCK), 0)
    nb_rows = km_ref.shape[1]
    blk_row = lax.broadcasted_iota(jnp.int32, (SUBLANES, ATT_BLOCK), 0)
    chan_all = lax.broadcasted_iota(jnp.int32, (LANES, seq), 0)
    n_tiles = q_ref.shape[2] // LANES
    tile = lambda tl: slice(tl * LANES, (tl + 1) * LANES)
    slot = lambda tl, hh: tl * HEADS_PER_TILE + hh

    for tl in range(n_tiles):
        k_all = k_ref[0, :, tile(tl)]
        v_all = v_ref[0, :, tile(tl)]
        means = [jnp.mean(k_all[j * ATT_BLOCK:(j + 1) * ATT_BLOCK].astype(F32), axis=0, keepdims=True)
                 for j in range(nb)]
        means.append(jnp.zeros((nb_rows - nb, LANES), F32))
        means = jnp.concatenate(means, axis=0)
        v_t = v_all.astype(F32).T
        for hh in range(HEADS_PER_TILE):
            own = (lane >= hh * HEAD_DIM) & (lane < (hh + 1) * HEAD_DIM)
            own_t = (chan_all >= hh * HEAD_DIM) & (chan_all < (hh + 1) * HEAD_DIM)
            kaug_ref[slot(tl, hh)] = jnp.where(own, k_all, kaux_ref[hh])
            vaug_t_ref[slot(tl, hh)] = jnp.where(own_t, v_t, 1.0).astype(BF16)
            km_ref[slot(tl, hh)] = jnp.where(own, means, 0.0).astype(BF16)

    units = [(i, tl, hh) for i in range(nb - 1, -1, -1)
             for tl in range(n_tiles) for hh in range(HEADS_PER_TILE)]
    n = len(units)
    q_aug, penalty, tiles, col_max, probs, acc, outs = ({} for _ in range(7))

    def prep(u):
        i, tl, hh = units[u]
        q_i = q_ref[0, i * ATT_BLOCK:(i + 1) * ATT_BLOCK, tile(tl)]
        own_q = (lane_q >= hh * HEAD_DIM) & (lane_q < (hh + 1) * HEAD_DIM)
        q_aug[u] = jnp.where(own_q, q_i, qaux_ref[tl, hh:hh + 1, :].astype(BF16))
        if i > MOBA_TOPK:
            gate = _dot_nt(km_ref[slot(tl, hh)], q_i)[:SUBLANES]
            beaten = jnp.zeros((SUBLANES, ATT_BLOCK), F32)
            for jp in range(i):
                g_jp = gate[jp:jp + 1, :]
                wins = (g_jp > gate) | ((g_jp == gate) & (jp < blk_row))
                beaten = beaten + jnp.where(wins, 1.0, 0.0)
            chosen = (blk_row < i) & (beaten < MOBA_TOPK)
            penalty[u] = jnp.where(chosen, 0.0, NEG)

    def qk(u):
        i, tl, hh = units[u]
        ts = []
        for j in range(i + 1):
            keys = kaug_ref[slot(tl, hh), j * ATT_BLOCK:(j + 1) * ATT_BLOCK, :]
            s = _dot_nt(keys, q_aug[u])
            if j == i:
                s = jnp.where(causal, s, NEG)
            elif i > MOBA_TOPK:
                s = s + penalty[u][j:j + 1, :]
            ts.append(s)
        m_el = ts[0]
        for s in ts[1:]:
            m_el = jnp.maximum(m_el, s)
        tiles[u] = ts
        col_max[u] = jnp.max(m_el, axis=0, keepdims=True)

    def expo(u):
        probs[u] = jnp.concatenate([jnp.exp2(s - col_max[u]).astype(BF16) for s in tiles[u]], axis=0)

    def pv(u):
        i, tl, hh = units[u]
        acc[u] = _dot(vaug_t_ref[slot(tl, hh), :, 0:(i + 1) * ATT_BLOCK], probs[u])

    def fin(u):
        i, tl, hh = units[u]
        spare0 = (1 - hh) * HEAD_DIM
        outs[u] = acc[u] / acc[u][spare0:spare0 + 1, :]
        if hh == HEADS_PER_TILE - 1:
            o_pair_t = jnp.where(chan_t < HEAD_DIM, outs[u - 1], outs[u])
            o_ref[0, i * ATT_BLOCK:(i + 1) * ATT_BLOCK, tile(tl)] = o_pair_t.T.astype(o_ref.dtype)

    stages = (prep, qk, expo, pv, fin)
    for t in range(n + (len(stages) - 1) * MOBA_SKEW):
        for k, stage in enumerate(stages):
            u = t - k * MOBA_SKEW
            if 0 <= u < n:
                stage(u)


def _bf16_terms(value, n_terms):
    terms, rest = [], float(value)
    for _ in range(n_terms):
        term = float(np.asarray(rest, np.float32).astype(jnp.bfloat16).astype(np.float32))
        terms.append(term)
        rest -= term
    return terms


def _moba_aux(seq):
    pos = np.arange(seq)
    lane = np.arange(LANES)
    pairs = MOBA_HEADS // HEADS_PER_TILE
    kaux = np.zeros((HEADS_PER_TILE, seq, LANES), np.float32)
    qaux = np.zeros((pairs, HEADS_PER_TILE, LANES), np.float32)
    for hh in range(HEADS_PER_TILE):
        j = lane - (1 - hh) * HEAD_DIM
        off_lanes = (j >= 0) & (j < SLOPE_TERMS)
        start_lanes = (j >= SLOPE_TERMS) & (j < 2 * SLOPE_TERMS)
        kaux[hh] = (np.where(off_lanes[None, :], (pos % MOBA_BLOCK)[:, None], 0)
                    + np.where(start_lanes[None, :], ((pos // MOBA_BLOCK) * MOBA_BLOCK)[:, None], 0))
        for p in range(pairs):
            slope = 2.0 ** (-8.0 * (p * HEADS_PER_TILE + hh + 1) / MOBA_HEADS)
            terms = _bf16_terms(slope * LOG2E, SLOPE_TERMS)
            for t, term in enumerate(terms):
                qaux[p, hh, (j == t) | (j == t + SLOPE_TERMS)] = term
    return jnp.asarray(kaux, BF16), jnp.asarray(qaux)


def _moba(proj3, q_col, k_col, v_col):
    b, seq, _ = proj3.shape
    nb = seq // MOBA_BLOCK
    assert seq % MOBA_BLOCK == 0 and nb <= SUBLANES
    pairs = MOBA_HEADS // HEADS_PER_TILE
    width = MOBA_TILES * LANES
    heads = MOBA_TILES * HEADS_PER_TILE
    kaux, qaux = _moba_aux(seq)
    blk = lambda c0: pl.BlockSpec((1, seq, width), lambda bi, p: (bi, 0, c0 + p))
    return pl.pallas_call(
        _moba_kernel,
        out_shape=jax.ShapeDtypeStruct((b, seq, MIX_WIDTH), BF16),
        grid=(b, pairs // MOBA_TILES),
        in_specs=[
            blk(q_col // width), blk(k_col // width), blk(v_col // width),
            _resident((HEADS_PER_TILE, seq, LANES)),
            pl.BlockSpec((MOBA_TILES, HEADS_PER_TILE, LANES), lambda bi, p: (p, 0, 0)),
        ],
        out_specs=pl.BlockSpec((1, seq, width), lambda bi, p: (bi, 0, p)),
        scratch_shapes=[
            pltpu.VMEM((heads, seq, LANES), BF16),
            pltpu.VMEM((heads, LANES, seq), BF16),
            pltpu.VMEM((heads, 2 * SUBLANES * (-(-nb // (2 * SUBLANES))), LANES), BF16),
        ],
        compiler_params=_params(2),
        name="moba",
    )(proj3, proj3, proj3, kaux, qaux)


SB_DEAD = 200.0
SB_SKEW = 2


def _sb_kernel(q_ref, k_ref, v_ref, tri_ref, o_ref, acc_ref, rest_ref):
    seq = q_ref.shape[1]
    nb = seq // ATT_BLOCK
    row_t = lax.broadcasted_iota(jnp.int32, (ATT_BLOCK, ATT_BLOCK), 0)
    col_t = lax.broadcasted_iota(jnp.int32, (ATT_BLOCK, ATT_BLOCK), 1)
    strict = col_t < row_t
    lane_q = lax.broadcasted_iota(jnp.int32, (ATT_BLOCK, LANES), 1)
    tri = tri_ref[...]
    key_rows = lambda j: slice(j * ATT_BLOCK, (j + 1) * ATT_BLOCK)

    def q_own(i, hh):
        q_i = q_ref[0, i * ATT_BLOCK:(i + 1) * ATT_BLOCK, :]
        own_q = (lane_q >= hh * HEAD_DIM) & (lane_q < (hh + 1) * HEAD_DIM)
        return jnp.where(own_q, q_i, jnp.zeros_like(q_i))

    def run_chains(chains, rest_of):
        n = len(chains)
        z, sp_bf, suffix, a = {}, {}, {}, {}
        outs, new_rests = [None] * n, [None] * n

        def scores(c):
            i, hh, j = chains[c]
            z[c] = _dot_nt(q_own(i, hh), k_ref[0, key_rows(j), :])

        def softplus(c):
            i, hh, j = chains[c]
            neg_abs = lax.bitcast_convert_type(
                lax.bitcast_convert_type(z[c], jnp.uint32) | jnp.uint32(0x80000000), F32)
            sp = jnp.maximum(z[c], 0.0) + jnp.log(1.0 + jnp.exp2(neg_abs)) * LOG2E
            if j == i:
                sp = jnp.where(strict, sp, 0.0)
            sp_bf[c] = sp.astype(BF16)

        def suffix_sum(c):
            suffix[c] = _dot(sp_bf[c], tri)

        def weights(c):
            i, hh, j = chains[c]
            rest = rest_of(c, new_rests)
            w = jnp.exp2(z[c] - suffix[c] - rest)
            if j == i:
                w = jnp.where(strict, w, 0.0)
            a[c] = w.astype(BF16)
            new_rests[c] = rest + suffix[c][:, 0:1]

        def values(c):
            i, hh, j = chains[c]
            outs[c] = _dot(a[c], v_ref[0, key_rows(j), :])

        stages = (scores, softplus, suffix_sum, weights, values)
        for t in range(n + (len(stages) - 1) * SB_SKEW):
            for k, stage in enumerate(stages):
                c = t - k * SB_SKEW
                if 0 <= c < n:
                    stage(c)
        return outs, new_rests

    slot = lambda i, hh: i * HEADS_PER_TILE + hh
    units = lambda d: [(i, hh, i - d) for i in range(d, nb) for hh in range(HEADS_PER_TILE)]

    near = units(0) + (units(1) if nb > 1 else [])
    where = {unit: c for c, unit in enumerate(near)}
    zero_rest = jnp.zeros((ATT_BLOCK, 1), F32)

    def near_rest(c, done):
        i, hh, j = near[c]
        return zero_rest if j == i else done[where[(i, hh, i)]]

    outs, rests = run_chains(near, near_rest)
    for (i, hh, j), o, r in zip(near, outs, rests):
        if j == i and (i, hh, i - 1) in where:
            continue
        total = o if j == i else o + outs[where[(i, hh, i)]]
        acc_ref[slot(i, hh)] = total
        rest_ref[slot(i, hh)] = jnp.broadcast_to(r, (ATT_BLOCK, LANES))

    def alive(d):
        low = rest_ref[slot(d, 0)]
        for s in range(slot(d, 0) + 1, nb * HEADS_PER_TILE):
            low = jnp.minimum(low, rest_ref[s])
        return jnp.min(low) < SB_DEAD

    def far_diagonal(d):
        far = units(d)
        outs, rests = run_chains(far, lambda c, _: rest_ref[slot(far[c][0], far[c][1])][:, 0:1])
        for (i, hh, _), o, r in zip(far, outs, rests):
            acc_ref[slot(i, hh)] = acc_ref[slot(i, hh)] + o
            rest_ref[slot(i, hh)] = jnp.broadcast_to(r, (ATT_BLOCK, LANES))

    def from_diagonal(d):
        far_diagonal(d)
        if d + 1 < nb:
            pl.when(alive(d + 1))(functools.partial(from_diagonal, d + 1))

    if nb > 2:
        pl.when(alive(2))(functools.partial(from_diagonal, 2))

    for i in range(nb):
        o_pair = jnp.where(lane_q < HEAD_DIM, acc_ref[slot(i, 0)], acc_ref[slot(i, 1)])
        o_ref[0, i * ATT_BLOCK:(i + 1) * ATT_BLOCK, :] = o_pair.astype(o_ref.dtype)


def _stick_breaking(proj3, q_col, k_col, v_col):
    b, seq, _ = proj3.shape
    assert seq % ATT_BLOCK == 0
    nb = seq // ATT_BLOCK
    pairs = SB_HEADS // HEADS_PER_TILE
    idx = jnp.arange(ATT_BLOCK)
    tri = (idx[:, None] >= idx[None, :]).astype(BF16)
    blk = lambda c0: pl.BlockSpec((1, seq, LANES), lambda bi, p: (bi, 0, c0 + p))
    state = pltpu.VMEM((nb * HEADS_PER_TILE, ATT_BLOCK, LANES), F32)
    return pl.pallas_call(
        _sb_kernel,
        out_shape=jax.ShapeDtypeStruct((b, seq, MIX_WIDTH), BF16),
        grid=(b, pairs),
        in_specs=[
            blk(q_col // LANES), blk(k_col // LANES), blk(v_col // LANES),
            _resident(tri.shape),
        ],
        out_specs=pl.BlockSpec((1, seq, LANES), lambda bi, p: (bi, 0, p)),
        scratch_shapes=[state, state],
        compiler_params=_params(2),
        name="stick_breaking",
    )(proj3, proj3, proj3, tri)


def _mem_kv_kernel(mem_ref, g_ref, w_ref, o_ref):
    h = _rms_scale(mem_ref[0], g_ref[...]).astype(BF16)
    o_ref[0] = _dot(h, w_ref[...]).astype(o_ref.dtype)


def _mem_kv(mem, g, w):
    b, m, d = mem.shape
    return pl.pallas_call(
        _mem_kv_kernel,
        out_shape=jax.ShapeDtypeStruct((b, m, w.shape[1]), BF16),
        grid=(b,),
        in_specs=[pl.BlockSpec((1, m, d), lambda i: (i, 0, 0)), _resident((1, d)), _resident(w.shape)],
        out_specs=pl.BlockSpec((1, m, w.shape[1]), lambda i: (i, 0, 0)),
        compiler_params=_params(1),
        name="mem_kv",
    )(mem, g, w)


def _mix_cross_kernel(a_ref, b_ref, ga_ref, gb_ref, x_ref, kv_ref, g_ref,
                      wa_ref, wb_ref, wo_ref, wq_ref, wxo_ref, o_ref):
    half = x_ref.shape[1] // 2
    rows = (slice(0, half), slice(half, 2 * half))
    ya, yb, merged, x1, q, probs, denom, attn = ({} for _ in range(8))
    head_cols = lambda hd: slice(hd * X_HEAD_DIM, (hd + 1) * X_HEAD_DIM)

    def branches(r):
        ya[r] = _dot(a_ref[0, rows[r], :], wa_ref[...])
        yb[r] = _dot(b_ref[0, rows[r], :], wb_ref[...])

    def merge(r):
        m = ga_ref[0, rows[r], :].astype(F32) * ya[r] + gb_ref[0, rows[r], :].astype(F32) * yb[r]
        merged[r] = m.astype(BF16)

    def mix_residual(r):
        x1[r] = x_ref[0, rows[r], :] + _dot(merged[r], wo_ref[...])

    def project(r):
        h = _rms_scale(x1[r], g_ref[...]).astype(BF16)
        q[r] = (_dot(h, wq_ref[...]) * (X_HEAD_DIM ** -0.5 * LOG2E)).astype(BF16)

    def scores(r):
        probs[r], denom[r] = [], []
        for hd in range(X_HEADS):
            s = _dot_nt(q[r][:, head_cols(hd)], kv_ref[0, :, head_cols(hd)])
            p = jnp.exp2(s - jnp.max(s, axis=1, keepdims=True))
            denom[r].append(jnp.sum(p, axis=1, keepdims=True))
            probs[r].append(p.astype(BF16))

    def values(r):
        heads = []
        for hd in range(X_HEADS):
            v_h = kv_ref[0, :, X_WIDTH + hd * X_HEAD_DIM:X_WIDTH + (hd + 1) * X_HEAD_DIM]
            heads.append((_dot(probs[r][hd], v_h) / denom[r][hd]).astype(BF16))
        attn[r] = jnp.concatenate(heads, axis=1)

    def cross_residual(r):
        o_ref[0, rows[r], :] = x1[r] + _dot(attn[r], wxo_ref[...])

    stages = (branches, merge, mix_residual, project, scores, values, cross_residual)
    for t in range(len(rows) + len(stages) - 1):
        for k, stage in enumerate(stages):
            if 0 <= t - k < len(rows):
                stage(t - k)


def _mix_cross(moba_o, sb_o, proj3, gate_col, x3, kv, g, wa, wb, wo, wq, wxo):
    b, seq, d = x3.shape
    m = kv.shape[1]
    rows = lambda w, c=0: pl.BlockSpec((1, WIDE_ROW_TILE, w), lambda bi, i: (bi, i, c))
    return pl.pallas_call(
        _mix_cross_kernel,
        out_shape=jax.ShapeDtypeStruct((b, seq, d), F32),
        grid=(b, seq // WIDE_ROW_TILE),
        in_specs=[
            rows(MIX_WIDTH), rows(MIX_WIDTH),
            rows(d, gate_col // d), rows(d, gate_col // d + 1),
            rows(d),
            pl.BlockSpec((1, m, kv.shape[2]), lambda bi, i: (bi, 0, 0)),
            _resident((1, d)),
            _resident(wa.shape), _resident(wb.shape), _resident(wo.shape),
            _resident(wq.shape), _resident(wxo.shape),
        ],
        out_specs=rows(d),
        compiler_params=_params(2),
        name="mix_cross",
    )(moba_o, sb_o, proj3, proj3, x3, kv, g, wa, wb, wo, wq, wxo)


def _mlp_kernel(x_ref, g_ref, wu_ref, wd_ref, gf_ref, o_ref, *, final_norm):
    x = x_ref[...]
    h = _rms_scale(x, g_ref[...]).astype(BF16)
    acc = x
    for c in range(wu_ref.shape[1] // COL_CHUNK):
        cols = slice(c * COL_CHUNK, (c + 1) * COL_CHUNK)
        u = jnp.maximum(_dot(h, wu_ref[:, cols]), 0.0)
        acc = acc + _dot((u * u).astype(BF16), wd_ref[cols, :])
    o_ref[...] = _rms_scale(acc, gf_ref[...]) if final_norm else acc


def _mlp(x2d, g, wu, wd, gf, final_norm):
    n, d = x2d.shape
    return pl.pallas_call(
        functools.partial(_mlp_kernel, final_norm=final_norm),
        out_shape=jax.ShapeDtypeStruct((n, d), F32),
        grid=(n // WIDE_ROW_TILE,),
        in_specs=[
            pl.BlockSpec((WIDE_ROW_TILE, d), lambda i: (i, 0)),
            _resident((1, d)), _resident(wu.shape), _resident(wd.shape), _resident((1, d)),
        ],
        out_specs=pl.BlockSpec((WIDE_ROW_TILE, d), lambda i: (i, 0)),
        compiler_params=_params(1),
        name="mlp_final",
    )(x2d, g, wu, wd, gf)


def kernel(x, mem, g_mix, w_in, w_br_moba, w_br_sb, w_out, g_cross, g_mem,
           w_xq, w_xkv, w_xo, g_mlp, w_up, w_down, g_final):
    b, seq, d = x.shape
    n = b * seq
    assert n % ROW_TILE == 0 and seq % WIDE_ROW_TILE == 0
    row = lambda v: v.reshape(1, -1).astype(F32)
    depth = g_mix.shape[0]
    x2d = x.reshape(n, d)
    for l in range(depth):
        proj = _in_proj(x2d, row(g_mix[l]), w_in[l].astype(BF16))
        proj3 = proj.reshape(b, seq, -1)
        moba_o = _moba(proj3, 0, MIX_WIDTH, 2 * MIX_WIDTH)
        sb_o = _stick_breaking(proj3, 3 * MIX_WIDTH, 4 * MIX_WIDTH, 5 * MIX_WIDTH)
        kv = _mem_kv(mem, row(g_mem[l]), w_xkv[l].astype(BF16))
        x3 = _mix_cross(moba_o, sb_o, proj3, 6 * MIX_WIDTH, x2d.reshape(b, seq, d), kv, row(g_cross[l]),
                        w_br_moba[l].astype(BF16), w_br_sb[l].astype(BF16), w_out[l].astype(BF16),
                        w_xq[l].astype(BF16), w_xo[l].astype(BF16))
        x2d = _mlp(x3.reshape(n, d), row(g_mlp[l]), w_up[l].astype(BF16), w_down[l].astype(BF16),
                   row(g_final), final_norm=(l == depth - 1))
    return x2d.reshape(b, seq, d)
```

```python
import functools
import math

import jax
import jax.numpy as jnp
import numpy as np
from jax import lax
from jax.experimental import pallas as pl
from jax.experimental.pallas import tpu as pltpu

F32 = jnp.float32
BF16 = jnp.bfloat16

HEAD_DIM = 64
MOBA_HEADS = 8
SB_HEADS = 8
MIX_WIDTH = MOBA_HEADS * HEAD_DIM
MOBA_BLOCK = 256
MOBA_TOPK = 3
X_HEADS = 4
X_HEAD_DIM = 128
X_WIDTH = X_HEADS * X_HEAD_DIM
RMS_EPS = 1e-6
NEG = -1e30
LOG2E = math.log2(math.e)
SLOPE_TERMS = 4

LANES = 128
SUBLANES = 8
HEADS_PER_TILE = LANES // HEAD_DIM
ATT_BLOCK = 256
ROW_TILE = 512
WIDE_ROW_TILE = 1024
COL_CHUNK = 512
VMEM_LIMIT = 56 * 1024 * 1024


def _params(n_axes):
    return pltpu.CompilerParams(
        dimension_semantics=("arbitrary",) * n_axes, vmem_limit_bytes=VMEM_LIMIT)


def _resident(shape):
    return pl.BlockSpec(shape, lambda *_: (0,) * len(shape), pipeline_mode=pl.Buffered(1))


def _rms_scale(x, g):
    y = x * lax.rsqrt(jnp.mean(x * x, axis=-1, keepdims=True) + RMS_EPS)
    return y * g


def _dot(a, b):
    return jnp.dot(a, b, preferred_element_type=F32)


def _dot_nt(a, b):
    return lax.dot_general(a, b, (((1,), (1,)), ((), ())), preferred_element_type=F32)


def _in_proj_kernel(x_ref, g_ref, w_ref, o_ref, *, q_chunks, gate_start):
    n_chunks = w_ref.shape[1] // COL_CHUNK
    half = x_ref.shape[0] // 2
    rows = (slice(0, half), slice(half, 2 * half))
    h = {}

    def norm(r):
        h[r] = _rms_scale(x_ref[rows[r], :], g_ref[...]).astype(BF16)

    def chunk(r, c):
        cols = slice(c * COL_CHUNK, (c + 1) * COL_CHUNK)
        acc = _dot(h[r], w_ref[:, cols])
        if c in q_chunks:
            acc = acc * (HEAD_DIM ** -0.5 * LOG2E)
        if c >= gate_start:
            acc = 1.0 / (1.0 + jnp.exp(-acc))
        o_ref[rows[r], cols] = acc.astype(o_ref.dtype)

    order = list(range(gate_start, n_chunks)) + list(range(gate_start))
    norm(0)
    chunk(0, order[0])
    norm(1)
    for k in range(1, n_chunks):
        chunk(1, order[k - 1])
        chunk(0, order[k])
    chunk(1, order[-1])


def _in_proj(x2d, g, w):
    n, d = x2d.shape
    cols = w.shape[1]
    q_chunks = (0, 3 * MIX_WIDTH // COL_CHUNK)
    gate_start = 6 * MIX_WIDTH // COL_CHUNK
    return pl.pallas_call(
        functools.partial(_in_proj_kernel, q_chunks=q_chunks, gate_start=gate_start),
        out_shape=jax.ShapeDtypeStruct((n, cols), BF16),
        grid=(n // ROW_TILE,),
        in_specs=[
            pl.BlockSpec((ROW_TILE, d), lambda i: (i, 0)),
            _resident((1, d)),
            _resident((d, cols)),
        ],
        out_specs=pl.BlockSpec((ROW_TILE, cols), lambda i: (i, 0)),
        compiler_params=_params(1),
        name="in_proj",
    )(x2d, g, w)


MOBA_SKEW = 1


def _moba_kernel(q_ref, k_ref, v_ref, kaux_ref, qaux_ref, o_ref,
                 kaug_ref, vaug_t_ref, km_ref):
    seq = q_ref.shape[1]
    nb = seq // ATT_BLOCK
    lane = lax.broadcasted_iota(jnp.int32, (1, LANES), 1)
    key_t = lax.broadcasted_iota(jnp.int32, (ATT_BLOCK, ATT_BLOCK), 0)
    qry_t = lax.broadcasted_iota(jnp.int32, (ATT_BLOCK, ATT_BLOCK), 1)
    causal = key_t <= qry_t
    lane_q = lax.broadcasted_iota(jnp.int32, (ATT_BLOCK, LANES), 1)
    chan_t = lax.broadcasted_iota(jnp.int32, (LANES, ATT_BLOCK), 0)
    nb_rows = km_ref.shape[1]
    blk_row = lax.broadcasted_iota(jnp.int32, (SUBLANES, ATT_BLOCK), 0)

    k_all = k_ref[0]
    v_all = v_ref[0]
    means = [jnp.mean(k_all[j * ATT_BLOCK:(j + 1) * ATT_BLOCK].astype(F32), axis=0, keepdims=True)
             for j in range(nb)]
    means.append(jnp.zeros((nb_rows - nb, LANES), F32))
    means = jnp.concatenate(means, axis=0)
    v_t = v_all.astype(F32).T
    chan_all = lax.broadcasted_iota(jnp.int32, (LANES, seq), 0)
    for hh in range(HEADS_PER_TILE):
        own = (lane >= hh * HEAD_DIM) & (lane < (hh + 1) * HEAD_DIM)
        own_t = (chan_all >= hh * HEAD_DIM) & (chan_all < (hh + 1) * HEAD_DIM)
        kaug_ref[hh] = jnp.where(own, k_all, kaux_ref[hh])
        vaug_t_ref[hh] = jnp.where(own_t, v_t, 1.0).astype(BF16)
        km_ref[hh] = jnp.where(own, means, 0.0).astype(BF16)

    units = [(i, hh) for i in range(nb - 1, -1, -1) for hh in range(HEADS_PER_TILE)]
    n = len(units)
    q_aug, penalty, tiles, col_max, probs, acc, outs = ({} for _ in range(7))

    def prep(u):
        i, hh = units[u]
        q_i = q_ref[0, i * ATT_BLOCK:(i + 1) * ATT_BLOCK, :]
        own_q = (lane_q >= hh * HEAD_DIM) & (lane_q < (hh + 1) * HEAD_DIM)
        q_aug[u] = jnp.where(own_q, q_i, qaux_ref[0, hh:hh + 1, :].astype(BF16))
        if i > MOBA_TOPK:
            gate = _dot_nt(km_ref[hh], q_i)[:SUBLANES]
            beaten = jnp.zeros((SUBLANES, ATT_BLOCK), F32)
            for jp in range(i):
                g_jp = gate[jp:jp + 1, :]
                wins = (g_jp > gate) | ((g_jp == gate) & (jp < blk_row))
                beaten = beaten + jnp.where(wins, 1.0, 0.0)
            chosen = (blk_row < i) & (beaten < MOBA_TOPK)
            penalty[u] = jnp.where(chosen, 0.0, NEG)

    def qk(u):
        i, hh = units[u]
        ts = []
        for j in range(i + 1):
            s = _dot_nt(kaug_ref[hh, j * ATT_BLOCK:(j + 1) * ATT_BLOCK, :], q_aug[u])
            if j == i:
                s = jnp.where(causal, s, NEG)
            elif i > MOBA_TOPK:
                s = s + penalty[u][j:j + 1, :]
            ts.append(s)
        m_el = ts[0]
        for s in ts[1:]:
            m_el = jnp.maximum(m_el, s)
        tiles[u] = ts
        col_max[u] = jnp.max(m_el, axis=0, keepdims=True)

    def expo(u):
        probs[u] = jnp.concatenate([jnp.exp2(s - col_max[u]).astype(BF16) for s in tiles[u]], axis=0)

    def pv(u):
        i, hh = units[u]
        acc[u] = _dot(vaug_t_ref[hh, :, 0:(i + 1) * ATT_BLOCK], probs[u])

    def fin(u):
        i, hh = units[u]
        spare0 = (1 - hh) * HEAD_DIM
        outs[u] = acc[u] / acc[u][spare0:spare0 + 1, :]
        if hh == HEADS_PER_TILE - 1:
            o_pair_t = jnp.where(chan_t < HEAD_DIM, outs[u - 1], outs[u])
            o_ref[0, i * ATT_BLOCK:(i + 1) * ATT_BLOCK, :] = o_pair_t.T.astype(o_ref.dtype)

    stages = (prep, qk, expo, pv, fin)
    for t in range(n + (len(stages) - 1) * MOBA_SKEW):
        for k, stage in enumerate(stages):
            u = t - k * MOBA_SKEW
            if 0 <= u < n:
                stage(u)


def _bf16_terms(value, n_terms):
    terms, rest = [], float(value)
    for _ in range(n_terms):
        term = float(np.asarray(rest, np.float32).astype(jnp.bfloat16).astype(np.float32))
        terms.append(term)
        rest -= term
    return terms


def _moba_aux(seq):
    pos = np.arange(seq)
    lane = np.arange(LANES)
    pairs = MOBA_HEADS // HEADS_PER_TILE
    kaux = np.zeros((HEADS_PER_TILE, seq, LANES), np.float32)
    qaux = np.zeros((pairs, HEADS_PER_TILE, LANES), np.float32)
    for hh in range(HEADS_PER_TILE):
        j = lane - (1 - hh) * HEAD_DIM
        off_lanes = (j >= 0) & (j < SLOPE_TERMS)
        start_lanes = (j >= SLOPE_TERMS) & (j < 2 * SLOPE_TERMS)
        kaux[hh] = (np.where(off_lanes[None, :], (pos % MOBA_BLOCK)[:, None], 0)
                    + np.where(start_lanes[None, :], ((pos // MOBA_BLOCK) * MOBA_BLOCK)[:, None], 0))
        for p in range(pairs):
            slope = 2.0 ** (-8.0 * (p * HEADS_PER_TILE + hh + 1) / MOBA_HEADS)
            terms = _bf16_terms(slope * LOG2E, SLOPE_TERMS)
            for t, term in enumerate(terms):
                qaux[p, hh, (j == t) | (j == t + SLOPE_TERMS)] = term
    return jnp.asarray(kaux, BF16), jnp.asarray(qaux)


def _moba(proj3, q_col, k_col, v_col):
    b, seq, _ = proj3.shape
    nb = seq // MOBA_BLOCK
    assert seq % MOBA_BLOCK == 0 and nb <= SUBLANES
    pairs = MOBA_HEADS // HEADS_PER_TILE
    kaux, qaux = _moba_aux(seq)
    blk = lambda c0: pl.BlockSpec((1, seq, LANES), lambda bi, p: (bi, 0, c0 + p))
    return pl.pallas_call(
        _moba_kernel,
        out_shape=jax.ShapeDtypeStruct((b, seq, MIX_WIDTH), BF16),
        grid=(b, pairs),
        in_specs=[
            blk(q_col // LANES), blk(k_col // LANES), blk(v_col // LANES),
            _resident((HEADS_PER_TILE, seq, LANES)),
            pl.BlockSpec((1, HEADS_PER_TILE, LANES), lambda bi, p: (p, 0, 0)),
        ],
        out_specs=pl.BlockSpec((1, seq, LANES), lambda bi, p: (bi, 0, p)),
        scratch_shapes=[
            pltpu.VMEM((HEADS_PER_TILE, seq, LANES), BF16),
            pltpu.VMEM((HEADS_PER_TILE, LANES, seq), BF16),
            pltpu.VMEM((HEADS_PER_TILE, 2 * SUBLANES * (-(-nb // (2 * SUBLANES))), LANES), BF16),
        ],
        compiler_params=_params(2),
        name="moba",
    )(proj3, proj3, proj3, kaux, qaux)


SB_DEAD = 200.0
SB_SKEW = 2


def _sb_kernel(q_ref, k_ref, v_ref, tri_ref, o_ref, acc_ref, rest_ref):
    seq = q_ref.shape[1]
    nb = seq // ATT_BLOCK
    row_t = lax.broadcasted_iota(jnp.int32, (ATT_BLOCK, ATT_BLOCK), 0)
    col_t = lax.broadcasted_iota(jnp.int32, (ATT_BLOCK, ATT_BLOCK), 1)
    strict = col_t < row_t
    lane_q = lax.broadcasted_iota(jnp.int32, (ATT_BLOCK, LANES), 1)
    tri = tri_ref[...]
    key_rows = lambda j: slice(j * ATT_BLOCK, (j + 1) * ATT_BLOCK)

    def q_own(i, hh):
        q_i = q_ref[0, i * ATT_BLOCK:(i + 1) * ATT_BLOCK, :]
        own_q = (lane_q >= hh * HEAD_DIM) & (lane_q < (hh + 1) * HEAD_DIM)
        return jnp.where(own_q, q_i, jnp.zeros_like(q_i))

    def run_chains(chains, rest_of):
        n = len(chains)
        z, sp_bf, suffix, a = {}, {}, {}, {}
        outs, new_rests = [None] * n, [None] * n

        def scores(c):
            i, hh, j = chains[c]
            z[c] = _dot_nt(q_own(i, hh), k_ref[0, key_rows(j), :])

        def softplus(c):
            i, hh, j = chains[c]
            neg_abs = lax.bitcast_convert_type(
                lax.bitcast_convert_type(z[c], jnp.uint32) | jnp.uint32(0x80000000), F32)
            sp = jnp.maximum(z[c], 0.0) + jnp.log(1.0 + jnp.exp2(neg_abs)) * LOG2E
            if j == i:
                sp = jnp.where(strict, sp, 0.0)
            sp_bf[c] = sp.astype(BF16)

        def suffix_sum(c):
            suffix[c] = _dot(sp_bf[c], tri)

        def weights(c):
            i, hh, j = chains[c]
            rest = rest_of(c, new_rests)
            w = jnp.exp2(z[c] - suffix[c] - rest)
            if j == i:
                w = jnp.where(strict, w, 0.0)
            a[c] = w.astype(BF16)
            new_rests[c] = rest + suffix[c][:, 0:1]

        def values(c):
            i, hh, j = chains[c]
            outs[c] = _dot(a[c], v_ref[0, key_rows(j), :])

        stages = (scores, softplus, suffix_sum, weights, values)
        for t in range(n + (len(stages) - 1) * SB_SKEW):
            for k, stage in enumerate(stages):
                c = t - k * SB_SKEW
                if 0 <= c < n:
                    stage(c)
        return outs, new_rests

    slot = lambda i, hh: i * HEADS_PER_TILE + hh
    units = lambda d: [(i, hh, i - d) for i in range(d, nb) for hh in range(HEADS_PER_TILE)]

    near = units(0) + (units(1) if nb > 1 else [])
    where = {unit: c for c, unit in enumerate(near)}
    zero_rest = jnp.zeros((ATT_BLOCK, 1), F32)

    def near_rest(c, done):
        i, hh, j = near[c]
        return zero_rest if j == i else done[where[(i, hh, i)]]

    outs, rests = run_chains(near, near_rest)
    for (i, hh, j), o, r in zip(near, outs, rests):
        if j == i and (i, hh, i - 1) in where:
            continue
        total = o if j == i else o + outs[where[(i, hh, i)]]
        acc_ref[slot(i, hh)] = total
        rest_ref[slot(i, hh)] = jnp.broadcast_to(r, (ATT_BLOCK, LANES))

    def alive(d):
        low = rest_ref[slot(d, 0)]
        for s in range(slot(d, 0) + 1, nb * HEADS_PER_TILE):
            low = jnp.minimum(low, rest_ref[s])
        return jnp.min(low) < SB_DEAD

    def far_diagonal(d):
        far = units(d)
        outs, rests = run_chains(far, lambda c, _: rest_ref[slot(far[c][0], far[c][1])][:, 0:1])
        for (i, hh, _), o, r in zip(far, outs, rests):
            acc_ref[slot(i, hh)] = acc_ref[slot(i, hh)] + o
            rest_ref[slot(i, hh)] = jnp.broadcast_to(r, (ATT_BLOCK, LANES))

    def from_diagonal(d):
        far_diagonal(d)
        if d + 1 < nb:
            pl.when(alive(d + 1))(functools.partial(from_diagonal, d + 1))

    if nb > 2:
        pl.when(alive(2))(functools.partial(from_diagonal, 2))

    for i in range(nb):
        o_pair = jnp.where(lane_q < HEAD_DIM, acc_ref[slot(i, 0)], acc_ref[slot(i, 1)])
        o_ref[0, i * ATT_BLOCK:(i + 1) * ATT_BLOCK, :] = o_pair.astype(o_ref.dtype)


def _stick_breaking(proj3, q_col, k_col, v_col):
    b, seq, _ = proj3.shape
    assert seq % ATT_BLOCK == 0
    nb = seq // ATT_BLOCK
    pairs = SB_HEADS // HEADS_PER_TILE
    idx = jnp.arange(ATT_BLOCK)
    tri = (idx[:, None] >= idx[None, :]).astype(BF16)
    blk = lambda c0: pl.BlockSpec((1, seq, LANES), lambda bi, p: (bi, 0, c0 + p))
    state = pltpu.VMEM((nb * HEADS_PER_TILE, ATT_BLOCK, LANES), F32)
    return pl.pallas_call(
        _sb_kernel,
        out_shape=jax.ShapeDtypeStruct((b, seq, MIX_WIDTH), BF16),
        grid=(b, pairs),
        in_specs=[
            blk(q_col // LANES), blk(k_col // LANES), blk(v_col // LANES),
            _resident(tri.shape),
        ],
        out_specs=pl.BlockSpec((1, seq, LANES), lambda bi, p: (bi, 0, p)),
        scratch_shapes=[state, state],
        compiler_params=_params(2),
        name="stick_breaking",
    )(proj3, proj3, proj3, tri)


def _mix_cross_kernel(a_ref, b_ref, ga_ref, gb_ref, x_ref, mem_ref, g_ref, gmem_ref,
                      wa_ref, wb_ref, wo_ref, wq_ref, wxo_ref, wkv_ref, o_ref, kv_ref):
    @pl.when(pl.program_id(1) == 0)
    def _():
        mem_n = _rms_scale(mem_ref[0], gmem_ref[...]).astype(BF16)
        kv_ref[...] = _dot(mem_n, wkv_ref[...]).astype(kv_ref.dtype)

    half = x_ref.shape[1] // 2
    rows = (slice(0, half), slice(half, 2 * half))
    ya, yb, merged, x1, q, probs, denom, attn = ({} for _ in range(8))
    head_cols = lambda hd: slice(hd * X_HEAD_DIM, (hd + 1) * X_HEAD_DIM)

    def branches(r):
        ya[r] = _dot(a_ref[0, rows[r], :], wa_ref[...])
        yb[r] = _dot(b_ref[0, rows[r], :], wb_ref[...])

    def merge(r):
        m = ga_ref[0, rows[r], :].astype(F32) * ya[r] + gb_ref[0, rows[r], :].astype(F32) * yb[r]
        merged[r] = m.astype(BF16)

    def mix_residual(r):
        x1[r] = x_ref[0, rows[r], :] + _dot(merged[r], wo_ref[...])

    def project(r):
        h = _rms_scale(x1[r], g_ref[...]).astype(BF16)
        q[r] = (_dot(h, wq_ref[...]) * (X_HEAD_DIM ** -0.5 * LOG2E)).astype(BF16)

    def scores(r):
        probs[r], denom[r] = [], []
        for hd in range(X_HEADS):
            s = _dot_nt(q[r][:, head_cols(hd)], kv_ref[:, head_cols(hd)])
            p = jnp.exp2(s - jnp.max(s, axis=1, keepdims=True))
            denom[r].append(jnp.sum(p, axis=1, keepdims=True))
            probs[r].append(p.astype(BF16))

    def values(r):
        heads = []
        for hd in range(X_HEADS):
            v_h = kv_ref[:, X_WIDTH + hd * X_HEAD_DIM:X_WIDTH + (hd + 1) * X_HEAD_DIM]
            heads.append((_dot(probs[r][hd], v_h) / denom[r][hd]).astype(BF16))
        attn[r] = jnp.concatenate(heads, axis=1)

    def cross_residual(r):
        o_ref[0, rows[r], :] = x1[r] + _dot(attn[r], wxo_ref[...])

    stages = (branches, merge, mix_residual, project, scores, values, cross_residual)
    for t in range(len(rows) + len(stages) - 1):
        for k, stage in enumerate(stages):
            if 0 <= t - k < len(rows):
                stage(t - k)


def _mix_cross(moba_o, sb_o, proj3, gate_col, x3, mem, g, g_mem, wa, wb, wo, wq, wxo, wkv):
    b, seq, d = x3.shape
    m = mem.shape[1]
    rows = lambda w, c=0: pl.BlockSpec((1, WIDE_ROW_TILE, w), lambda bi, i: (bi, i, c))
    return pl.pallas_call(
        _mix_cross_kernel,
        out_shape=jax.ShapeDtypeStruct((b, seq, d), F32),
        grid=(b, seq // WIDE_ROW_TILE),
        in_specs=[
            rows(MIX_WIDTH), rows(MIX_WIDTH),
            rows(d, gate_col // d), rows(d, gate_col // d + 1),
            rows(d),
            pl.BlockSpec((1, m, d), lambda bi, i: (bi, 0, 0)),
            _resident((1, d)), _resident((1, d)),
            _resident(wa.shape), _resident(wb.shape), _resident(wo.shape),
            _resident(wq.shape), _resident(wxo.shape), _resident(wkv.shape),
        ],
        out_specs=rows(d),
        scratch_shapes=[pltpu.VMEM((m, wkv.shape[1]), BF16)],
        compiler_params=_params(2),
        name="mix_cross",
    )(moba_o, sb_o, proj3, proj3, x3, mem, g, g_mem, wa, wb, wo, wq, wxo, wkv)


def _mlp_kernel(x_ref, g_ref, wu_ref, wd_ref, gf_ref, o_ref, *, final_norm):
    x = x_ref[...]
    h = _rms_scale(x, g_ref[...]).astype(BF16)
    acc = x
    for c in range(wu_ref.shape[1] // COL_CHUNK):
        cols = slice(c * COL_CHUNK, (c + 1) * COL_CHUNK)
        u = jnp.maximum(_dot(h, wu_ref[:, cols]), 0.0)
        acc = acc + _dot((u * u).astype(BF16), wd_ref[cols, :])
    o_ref[...] = _rms_scale(acc, gf_ref[...]) if final_norm else acc


def _mlp(x2d, g, wu, wd, gf, final_norm):
    n, d = x2d.shape
    return pl.pallas_call(
        functools.partial(_mlp_kernel, final_norm=final_norm),
        out_shape=jax.ShapeDtypeStruct((n, d), F32),
        grid=(n // WIDE_ROW_TILE,),
        in_specs=[
            pl.BlockSpec((WIDE_ROW_TILE, d), lambda i: (i, 0)),
            _resident((1, d)), _resident(wu.shape), _resident(wd.shape), _resident((1, d)),
        ],
        out_specs=pl.BlockSpec((WIDE_ROW_TILE, d), lambda i: (i, 0)),
        compiler_params=_params(1),
        name="mlp_final",
    )(x2d, g, wu, wd, gf)


def kernel(x, mem, g_mix, w_in, w_br_moba, w_br_sb, w_out, g_cross, g_mem,
           w_xq, w_xkv, w_xo, g_mlp, w_up, w_down, g_final):
    b, seq, d = x.shape
    n = b * seq
    assert n % ROW_TILE == 0 and seq % WIDE_ROW_TILE == 0
    row = lambda v: v.reshape(1, -1).astype(F32)
    depth = g_mix.shape[0]
    x2d = x.reshape(n, d)
    for l in range(depth):
        proj = _in_proj(x2d, row(g_mix[l]), w_in[l].astype(BF16))
        proj3 = proj.reshape(b, seq, -1)
        moba_o = _moba(proj3, 0, MIX_WIDTH, 2 * MIX_WIDTH)
        sb_o = _stick_breaking(proj3, 3 * MIX_WIDTH, 4 * MIX_WIDTH, 5 * MIX_WIDTH)
        x3 = _mix_cross(moba_o, sb_o, proj3, 6 * MIX_WIDTH, x2d.reshape(b, seq, d), mem,
                        row(g_cross[l]), row(g_mem[l]),
                        w_br_moba[l].astype(BF16), w_br_sb[l].astype(BF16), w_out[l].astype(BF16),
                        w_xq[l].astype(BF16), w_xo[l].astype(BF16), w_xkv[l].astype(BF16))
        x2d = _mlp(x3.reshape(n, d), row(g_mlp[l]), w_up[l].astype(BF16), w_down[l].astype(BF16),
                   row(g_final), final_norm=(l == depth - 1))
    return x2d.reshape(b, seq, d)
```

```python
import functools
import math

import jax
import jax.numpy as jnp
import numpy as np
from jax import lax
from jax.experimental import pallas as pl
from jax.experimental.pallas import tpu as pltpu

F32 = jnp.float32
BF16 = jnp.bfloat16

HEAD_DIM = 64
MOBA_HEADS = 8
SB_HEADS = 8
MIX_WIDTH = MOBA_HEADS * HEAD_DIM
MOBA_BLOCK = 256
MOBA_TOPK = 3
X_HEADS = 4
X_HEAD_DIM = 128
X_WIDTH = X_HEADS * X_HEAD_DIM
RMS_EPS = 1e-6
NEG = -1e30
LOG2E = math.log2(math.e)
SLOPE_TERMS = 4

LANES = 128
SUBLANES = 8
HEADS_PER_TILE = LANES // HEAD_DIM
ATT_BLOCK = 256
ROW_TILE = 512
WIDE_ROW_TILE = 1024
COL_CHUNK = 512
VMEM_LIMIT = 56 * 1024 * 1024


def _params(n_axes):
    return pltpu.CompilerParams(
        dimension_semantics=("arbitrary",) * n_axes, vmem_limit_bytes=VMEM_LIMIT)


def _resident(shape):
    return pl.BlockSpec(shape, lambda *_: (0,) * len(shape), pipeline_mode=pl.Buffered(1))


def _rms_scale(x, g):
    y = x * lax.rsqrt(jnp.mean(x * x, axis=-1, keepdims=True) + RMS_EPS)
    return y * g


def _dot(a, b):
    return jnp.dot(a, b, preferred_element_type=F32)


def _dot_nt(a, b):
    return lax.dot_general(a, b, (((1,), (1,)), ((), ())), preferred_element_type=F32)


def _in_proj_kernel(x_ref, g_ref, w_ref, o_ref, *, q_chunks, gate_start):
    n_chunks = w_ref.shape[1] // COL_CHUNK
    half = x_ref.shape[0] // 2
    rows = (slice(0, half), slice(half, 2 * half))
    h, w_bf = {}, {}

    def norm(r):
        h[r] = _rms_scale(x_ref[rows[r], :], g_ref[...]).astype(BF16)

    def chunk(r, c):
        cols = slice(c * COL_CHUNK, (c + 1) * COL_CHUNK)
        if c not in w_bf:
            w_bf[c] = w_ref[:, cols].astype(BF16)
        acc = _dot(h[r], w_bf[c])
        if c in q_chunks:
            acc = acc * (HEAD_DIM ** -0.5 * LOG2E)
        if c >= gate_start:
            acc = 1.0 / (1.0 + jnp.exp(-acc))
        o_ref[rows[r], cols] = acc.astype(o_ref.dtype)

    order = list(range(gate_start, n_chunks)) + list(range(gate_start))
    norm(0)
    chunk(0, order[0])
    norm(1)
    for k in range(1, n_chunks):
        chunk(1, order[k - 1])
        chunk(0, order[k])
    chunk(1, order[-1])


def _in_proj(x2d, g, w):
    n, d = x2d.shape
    cols = w.shape[1]
    q_chunks = (0, 3 * MIX_WIDTH // COL_CHUNK)
    gate_start = 6 * MIX_WIDTH // COL_CHUNK
    return pl.pallas_call(
        functools.partial(_in_proj_kernel, q_chunks=q_chunks, gate_start=gate_start),
        out_shape=jax.ShapeDtypeStruct((n, cols), BF16),
        grid=(n // ROW_TILE,),
        in_specs=[
            pl.BlockSpec((ROW_TILE, d), lambda i: (i, 0)),
            _resident((1, d)),
            _resident((d, cols)),
        ],
        out_specs=pl.BlockSpec((ROW_TILE, cols), lambda i: (i, 0)),
        compiler_params=_params(1),
        name="in_proj",
    )(x2d, g, w)


MOBA_SKEW = 1


def _moba_kernel(q_ref, k_ref, v_ref, kaux_ref, qaux_ref, o_ref,
                 kaug_ref, vaug_t_ref, km_ref):
    seq = q_ref.shape[1]
    nb = seq // ATT_BLOCK
    lane = lax.broadcasted_iota(jnp.int32, (1, LANES), 1)
    key_t = lax.broadcasted_iota(jnp.int32, (ATT_BLOCK, ATT_BLOCK), 0)
    qry_t = lax.broadcasted_iota(jnp.int32, (ATT_BLOCK, ATT_BLOCK), 1)
    causal = key_t <= qry_t
    lane_q = lax.broadcasted_iota(jnp.int32, (ATT_BLOCK, LANES), 1)
    chan_t = lax.broadcasted_iota(jnp.int32, (LANES, ATT_BLOCK), 0)
    nb_rows = km_ref.shape[1]
    blk_row = lax.broadcasted_iota(jnp.int32, (SUBLANES, ATT_BLOCK), 0)

    k_all = k_ref[0]
    v_all = v_ref[0]
    means = [jnp.mean(k_all[j * ATT_BLOCK:(j + 1) * ATT_BLOCK].astype(F32), axis=0, keepdims=True)
             for j in range(nb)]
    means.append(jnp.zeros((nb_rows - nb, LANES), F32))
    means = jnp.concatenate(means, axis=0)
    v_t = v_all.astype(F32).T
    chan_all = lax.broadcasted_iota(jnp.int32, (LANES, seq), 0)
    for hh in range(HEADS_PER_TILE):
        own = (lane >= hh * HEAD_DIM) & (lane < (hh + 1) * HEAD_DIM)
        own_t = (chan_all >= hh * HEAD_DIM) & (chan_all < (hh + 1) * HEAD_DIM)
        kaug_ref[hh] = jnp.where(own, k_all, kaux_ref[hh])
        vaug_t_ref[hh] = jnp.where(own_t, v_t, 1.0).astype(BF16)
        km_ref[hh] = jnp.where(own, means, 0.0).astype(BF16)

    units = [(i, hh) for i in range(nb - 1, -1, -1) for hh in range(HEADS_PER_TILE)]
    n = len(units)
    q_aug, penalty, tiles, col_max, probs, acc, outs = ({} for _ in range(7))

    def prep(u):
        i, hh = units[u]
        q_i = q_ref[0, i * ATT_BLOCK:(i + 1) * ATT_BLOCK, :]
        own_q = (lane_q >= hh * HEAD_DIM) & (lane_q < (hh + 1) * HEAD_DIM)
        q_aug[u] = jnp.where(own_q, q_i, qaux_ref[0, hh:hh + 1, :].astype(BF16))
        if i > MOBA_TOPK:
            gate = _dot_nt(km_ref[hh], q_i)[:SUBLANES]
            beaten = jnp.zeros((SUBLANES, ATT_BLOCK), F32)
            for jp in range(i):
                g_jp = gate[jp:jp + 1, :]
                wins = (g_jp > gate) | ((g_jp == gate) & (jp < blk_row))
                beaten = beaten + jnp.where(wins, 1.0, 0.0)
            chosen = (blk_row < i) & (beaten < MOBA_TOPK)
            penalty[u] = jnp.where(chosen, 0.0, NEG)

    def qk(u):
        i, hh = units[u]
        ts = []
        for j in range(i + 1):
            s = _dot_nt(kaug_ref[hh, j * ATT_BLOCK:(j + 1) * ATT_BLOCK, :], q_aug[u])
            if j == i:
                s = jnp.where(causal, s, NEG)
            elif i > MOBA_TOPK:
                s = s + penalty[u][j:j + 1, :]
            ts.append(s)
        m_el = ts[0]
        for s in ts[1:]:
            m_el = jnp.maximum(m_el, s)
        tiles[u] = ts
        col_max[u] = jnp.max(m_el, axis=0, keepdims=True)

    def expo(u):
        probs[u] = jnp.concatenate([jnp.exp2(s - col_max[u]).astype(BF16) for s in tiles[u]], axis=0)

    def pv(u):
        i, hh = units[u]
        acc[u] = _dot(vaug_t_ref[hh, :, 0:(i + 1) * ATT_BLOCK], probs[u])

    def fin(u):
        i, hh = units[u]
        spare0 = (1 - hh) * HEAD_DIM
        outs[u] = acc[u] / acc[u][spare0:spare0 + 1, :]
        if hh == HEADS_PER_TILE - 1:
            o_pair_t = jnp.where(chan_t < HEAD_DIM, outs[u - 1], outs[u])
            o_ref[0, i * ATT_BLOCK:(i + 1) * ATT_BLOCK, :] = o_pair_t.T.astype(o_ref.dtype)

    stages = (prep, qk, expo, pv, fin)
    for t in range(n + (len(stages) - 1) * MOBA_SKEW):
        for k, stage in enumerate(stages):
            u = t - k * MOBA_SKEW
            if 0 <= u < n:
                stage(u)


def _bf16_terms(value, n_terms):
    terms, rest = [], float(value)
    for _ in range(n_terms):
        term = float(np.asarray(rest, np.float32).astype(jnp.bfloat16).astype(np.float32))
        terms.append(term)
        rest -= term
    return terms


def _moba_aux(seq):
    pos = np.arange(seq)
    lane = np.arange(LANES)
    pairs = MOBA_HEADS // HEADS_PER_TILE
    kaux = np.zeros((HEADS_PER_TILE, seq, LANES), np.float32)
    qaux = np.zeros((pairs, HEADS_PER_TILE, LANES), np.float32)
    for hh in range(HEADS_PER_TILE):
        j = lane - (1 - hh) * HEAD_DIM
        off_lanes = (j >= 0) & (j < SLOPE_TERMS)
        start_lanes = (j >= SLOPE_TERMS) & (j < 2 * SLOPE_TERMS)
        kaux[hh] = (np.where(off_lanes[None, :], (pos % MOBA_BLOCK)[:, None], 0)
                    + np.where(start_lanes[None, :], ((pos // MOBA_BLOCK) * MOBA_BLOCK)[:, None], 0))
        for p in range(pairs):
            slope = 2.0 ** (-8.0 * (p * HEADS_PER_TILE + hh + 1) / MOBA_HEADS)
            terms = _bf16_terms(slope * LOG2E, SLOPE_TERMS)
            for t, term in enumerate(terms):
                qaux[p, hh, (j == t) | (j == t + SLOPE_TERMS)] = term
    return jnp.asarray(kaux, BF16), jnp.asarray(qaux)


def _moba(proj3, q_col, k_col, v_col):
    b, seq, _ = proj3.shape
    nb = seq // MOBA_BLOCK
    assert seq % MOBA_BLOCK == 0 and nb <= SUBLANES
    pairs = MOBA_HEADS // HEADS_PER_TILE
    kaux, qaux = _moba_aux(seq)
    blk = lambda c0: pl.BlockSpec((1, seq, LANES), lambda bi, p: (bi, 0, c0 + p))
    return pl.pallas_call(
        _moba_kernel,
        out_shape=jax.ShapeDtypeStruct((b, seq, MIX_WIDTH), BF16),
        grid=(b, pairs),
        in_specs=[
            blk(q_col // LANES), blk(k_col // LANES), blk(v_col // LANES),
            _resident((HEADS_PER_TILE, seq, LANES)),
            pl.BlockSpec((1, HEADS_PER_TILE, LANES), lambda bi, p: (p, 0, 0)),
        ],
        out_specs=pl.BlockSpec((1, seq, LANES), lambda bi, p: (bi, 0, p)),
        scratch_shapes=[
            pltpu.VMEM((HEADS_PER_TILE, seq, LANES), BF16),
            pltpu.VMEM((HEADS_PER_TILE, LANES, seq), BF16),
            pltpu.VMEM((HEADS_PER_TILE, 2 * SUBLANES * (-(-nb // (2 * SUBLANES))), LANES), BF16),
        ],
        compiler_params=_params(2),
        name="moba",
    )(proj3, proj3, proj3, kaux, qaux)


SB_DEAD = 200.0
SB_SKEW = 2


def _sb_kernel(q_ref, k_ref, v_ref, tri_ref, o_ref, acc_ref, rest_ref):
    seq = q_ref.shape[1]
    nb = seq // ATT_BLOCK
    row_t = lax.broadcasted_iota(jnp.int32, (ATT_BLOCK, ATT_BLOCK), 0)
    col_t = lax.broadcasted_iota(jnp.int32, (ATT_BLOCK, ATT_BLOCK), 1)
    strict = col_t < row_t
    lane_q = lax.broadcasted_iota(jnp.int32, (ATT_BLOCK, LANES), 1)
    tri = tri_ref[...]
    key_rows = lambda j: slice(j * ATT_BLOCK, (j + 1) * ATT_BLOCK)

    def q_own(i, hh):
        q_i = q_ref[0, i * ATT_BLOCK:(i + 1) * ATT_BLOCK, :]
        own_q = (lane_q >= hh * HEAD_DIM) & (lane_q < (hh + 1) * HEAD_DIM)
        return jnp.where(own_q, q_i, jnp.zeros_like(q_i))

    def run_chains(chains, rest_of):
        n = len(chains)
        z, sp_bf, suffix, a = {}, {}, {}, {}
        outs, new_rests = [None] * n, [None] * n

        def scores(c):
            i, hh, j = chains[c]
            z[c] = _dot_nt(q_own(i, hh), k_ref[0, key_rows(j), :])

        def softplus(c):
            i, hh, j = chains[c]
            neg_abs = lax.bitcast_convert_type(
                lax.bitcast_convert_type(z[c], jnp.uint32) | jnp.uint32(0x80000000), F32)
            sp = jnp.maximum(z[c], 0.0) + jnp.log(1.0 + jnp.exp2(neg_abs)) * LOG2E
            if j == i:
                sp = jnp.where(strict, sp, 0.0)
            sp_bf[c] = sp.astype(BF16)

        def suffix_sum(c):
            suffix[c] = _dot(sp_bf[c], tri)

        def weights(c):
            i, hh, j = chains[c]
            rest = rest_of(c, new_rests)
            w = jnp.exp2(z[c] - suffix[c] - rest)
            if j == i:
                w = jnp.where(strict, w, 0.0)
            a[c] = w.astype(BF16)
            new_rests[c] = rest + suffix[c][:, 0:1]

        def values(c):
            i, hh, j = chains[c]
            outs[c] = _dot(a[c], v_ref[0, key_rows(j), :])

        stages = (scores, softplus, suffix_sum, weights, values)
        for t in range(n + (len(stages) - 1) * SB_SKEW):
            for k, stage in enumerate(stages):
                c = t - k * SB_SKEW
                if 0 <= c < n:
                    stage(c)
        return outs, new_rests

    slot = lambda i, hh: i * HEADS_PER_TILE + hh
    units = lambda d: [(i, hh, i - d) for i in range(d, nb) for hh in range(HEADS_PER_TILE)]

    near = units(0) + (units(1) if nb > 1 else [])
    where = {unit: c for c, unit in enumerate(near)}
    zero_rest = jnp.zeros((ATT_BLOCK, 1), F32)

    def near_rest(c, done):
        i, hh, j = near[c]
        return zero_rest if j == i else done[where[(i, hh, i)]]

    outs, rests = run_chains(near, near_rest)
    for (i, hh, j), o, r in zip(near, outs, rests):
        if j == i and (i, hh, i - 1) in where:
            continue
        total = o if j == i else o + outs[where[(i, hh, i)]]
        acc_ref[slot(i, hh)] = total
        rest_ref[slot(i, hh)] = jnp.broadcast_to(r, (ATT_BLOCK, LANES))

    def alive(d):
        low = rest_ref[slot(d, 0)]
        for s in range(slot(d, 0) + 1, nb * HEADS_PER_TILE):
            low = jnp.minimum(low, rest_ref[s])
        return jnp.min(low) < SB_DEAD

    def far_diagonal(d):
        far = units(d)
        outs, rests = run_chains(far, lambda c, _: rest_ref[slot(far[c][0], far[c][1])][:, 0:1])
        for (i, hh, _), o, r in zip(far, outs, rests):
            acc_ref[slot(i, hh)] = acc_ref[slot(i, hh)] + o
            rest_ref[slot(i, hh)] = jnp.broadcast_to(r, (ATT_BLOCK, LANES))

    def from_diagonal(d):
        far_diagonal(d)
        if d + 1 < nb:
            pl.when(alive(d + 1))(functools.partial(from_diagonal, d + 1))

    if nb > 2:
        pl.when(alive(2))(functools.partial(from_diagonal, 2))

    for i in range(nb):
        o_pair = jnp.where(lane_q < HEAD_DIM, acc_ref[slot(i, 0)], acc_ref[slot(i, 1)])
        o_ref[0, i * ATT_BLOCK:(i + 1) * ATT_BLOCK, :] = o_pair.astype(o_ref.dtype)


def _stick_breaking(proj3, q_col, k_col, v_col):
    b, seq, _ = proj3.shape
    assert seq % ATT_BLOCK == 0
    nb = seq // ATT_BLOCK
    pairs = SB_HEADS // HEADS_PER_TILE
    idx = jnp.arange(ATT_BLOCK)
    tri = (idx[:, None] >= idx[None, :]).astype(BF16)
    blk = lambda c0: pl.BlockSpec((1, seq, LANES), lambda bi, p: (bi, 0, c0 + p))
    state = pltpu.VMEM((nb * HEADS_PER_TILE, ATT_BLOCK, LANES), F32)
    return pl.pallas_call(
        _sb_kernel,
        out_shape=jax.ShapeDtypeStruct((b, seq, MIX_WIDTH), BF16),
        grid=(b, pairs),
        in_specs=[
            blk(q_col // LANES), blk(k_col // LANES), blk(v_col // LANES),
            _resident(tri.shape),
        ],
        out_specs=pl.BlockSpec((1, seq, LANES), lambda bi, p: (bi, 0, p)),
        scratch_shapes=[state, state],
        compiler_params=_params(2),
        name="stick_breaking",
    )(proj3, proj3, proj3, tri)


def _mix_cross_kernel(a_ref, b_ref, ga_ref, gb_ref, x_ref, mem_ref, g_ref, gmem_ref,
                      wa_ref, wb_ref, wo_ref, wq_ref, wxo_ref, wkv_ref, o_ref, kv_ref):
    @pl.when(pl.program_id(1) == 0)
    def _():
        mem_n = _rms_scale(mem_ref[0], gmem_ref[...]).astype(BF16)
        kv_ref[...] = _dot(mem_n, wkv_ref[...].astype(BF16)).astype(kv_ref.dtype)

    half = x_ref.shape[1] // 2
    rows = (slice(0, half), slice(half, 2 * half))
    ya, yb, merged, x1, q, probs, denom, attn, cast = ({} for _ in range(9))

    def w16(ref):
        if id(ref) not in cast:
            cast[id(ref)] = ref[...].astype(BF16)
        return cast[id(ref)]
    head_cols = lambda hd: slice(hd * X_HEAD_DIM, (hd + 1) * X_HEAD_DIM)

    def branches(r):
        ya[r] = _dot(a_ref[0, rows[r], :], w16(wa_ref))
        yb[r] = _dot(b_ref[0, rows[r], :], w16(wb_ref))

    def merge(r):
        m = ga_ref[0, rows[r], :].astype(F32) * ya[r] + gb_ref[0, rows[r], :].astype(F32) * yb[r]
        merged[r] = m.astype(BF16)

    def mix_residual(r):
        x1[r] = x_ref[0, rows[r], :] + _dot(merged[r], w16(wo_ref))

    def project(r):
        h = _rms_scale(x1[r], g_ref[...]).astype(BF16)
        q[r] = (_dot(h, w16(wq_ref)) * (X_HEAD_DIM ** -0.5 * LOG2E)).astype(BF16)

    def scores(r):
        probs[r], denom[r] = [], []
        for hd in range(X_HEADS):
            s = _dot_nt(q[r][:, head_cols(hd)], kv_ref[:, head_cols(hd)])
            p = jnp.exp2(s - jnp.max(s, axis=1, keepdims=True))
            denom[r].append(jnp.sum(p, axis=1, keepdims=True))
            probs[r].append(p.astype(BF16))

    def values(r):
        heads = []
        for hd in range(X_HEADS):
            v_h = kv_ref[:, X_WIDTH + hd * X_HEAD_DIM:X_WIDTH + (hd + 1) * X_HEAD_DIM]
            heads.append((_dot(probs[r][hd], v_h) / denom[r][hd]).astype(BF16))
        attn[r] = jnp.concatenate(heads, axis=1)

    def cross_residual(r):
        o_ref[0, rows[r], :] = x1[r] + _dot(attn[r], w16(wxo_ref))

    stages = (branches, merge, mix_residual, project, scores, values, cross_residual)
    for t in range(len(rows) + len(stages) - 1):
        for k, stage in enumerate(stages):
            if 0 <= t - k < len(rows):
                stage(t - k)


def _mix_cross(moba_o, sb_o, proj3, gate_col, x3, mem, g, g_mem, wa, wb, wo, wq, wxo, wkv):
    b, seq, d = x3.shape
    m = mem.shape[1]
    rows = lambda w, c=0: pl.BlockSpec((1, WIDE_ROW_TILE, w), lambda bi, i: (bi, i, c))
    return pl.pallas_call(
        _mix_cross_kernel,
        out_shape=jax.ShapeDtypeStruct((b, seq, d), F32),
        grid=(b, seq // WIDE_ROW_TILE),
        in_specs=[
            rows(MIX_WIDTH), rows(MIX_WIDTH),
            rows(d, gate_col // d), rows(d, gate_col // d + 1),
            rows(d),
            pl.BlockSpec((1, m, d), lambda bi, i: (bi, 0, 0)),
            _resident((1, d)), _resident((1, d)),
            _resident(wa.shape), _resident(wb.shape), _resident(wo.shape),
            _resident(wq.shape), _resident(wxo.shape), _resident(wkv.shape),
        ],
        out_specs=rows(d),
        scratch_shapes=[pltpu.VMEM((m, wkv.shape[1]), BF16)],
        compiler_params=_params(2),
        name="mix_cross",
    )(moba_o, sb_o, proj3, proj3, x3, mem, g, g_mem, wa, wb, wo, wq, wxo, wkv)


def _mlp_kernel(x_ref, g_ref, wu_ref, wd_ref, gf_ref, o_ref, *, final_norm):
    x = x_ref[...]
    h = _rms_scale(x, g_ref[...]).astype(BF16)
    acc = x
    for c in range(wu_ref.shape[1] // COL_CHUNK):
        cols = slice(c * COL_CHUNK, (c + 1) * COL_CHUNK)
        u = jnp.maximum(_dot(h, wu_ref[:, cols].astype(BF16)), 0.0)
        acc = acc + _dot((u * u).astype(BF16), wd_ref[cols, :].astype(BF16))
    o_ref[...] = _rms_scale(acc, gf_ref[...]) if final_norm else acc


def _mlp(x2d, g, wu, wd, gf, final_norm):
    n, d = x2d.shape
    return pl.pallas_call(
        functools.partial(_mlp_kernel, final_norm=final_norm),
        out_shape=jax.ShapeDtypeStruct((n, d), F32),
        grid=(n // WIDE_ROW_TILE,),
        in_specs=[
            pl.BlockSpec((WIDE_ROW_TILE, d), lambda i: (i, 0)),
            _resident((1, d)), _resident(wu.shape), _resident(wd.shape), _resident((1, d)),
        ],
        out_specs=pl.BlockSpec((WIDE_ROW_TILE, d), lambda i: (i, 0)),
        compiler_params=_params(1),
        name="mlp_final",
    )(x2d, g, wu, wd, gf)


def kernel(x, mem, g_mix, w_in, w_br_moba, w_br_sb, w_out, g_cross, g_mem,
           w_xq, w_xkv, w_xo, g_mlp, w_up, w_down, g_final):
    b, seq, d = x.shape
    n = b * seq
    assert n % ROW_TILE == 0 and seq % WIDE_ROW_TILE == 0
    row = lambda v: v.reshape(1, -1).astype(F32)
    depth = g_mix.shape[0]
    x2d = x.reshape(n, d)
    for l in range(depth):
        proj = _in_proj(x2d, row(g_mix[l]), w_in[l])
        proj3 = proj.reshape(b, seq, -1)
        moba_o = _moba(proj3, 0, MIX_WIDTH, 2 * MIX_WIDTH)
        sb_o = _stick_breaking(proj3, 3 * MIX_WIDTH, 4 * MIX_WIDTH, 5 * MIX_WIDTH)
        x3 = _mix_cross(moba_o, sb_o, proj3, 6 * MIX_WIDTH, x2d.reshape(b, seq, d), mem,
                        row(g_cross[l]), row(g_mem[l]),
                        w_br_moba[l], w_br_sb[l], w_out[l], w_xq[l], w_xo[l], w_xkv[l])
        x2d = _mlp(x3.reshape(n, d), row(g_mlp[l]), w_up[l], w_down[l],
                   row(g_final), final_norm=(l == depth - 1))
    return x2d.reshape(b, seq, d)
```

```python
import functools
import math

import jax
import jax.numpy as jnp
import numpy as np
from jax import lax
from jax.experimental import pallas as pl
from jax.experimental.pallas import tpu as pltpu

F32 = jnp.float32
BF16 = jnp.bfloat16

HEAD_DIM = 64
MOBA_HEADS = 8
SB_HEADS = 8
MIX_WIDTH = MOBA_HEADS * HEAD_DIM
MOBA_BLOCK = 256
MOBA_TOPK = 3
X_HEADS = 4
X_HEAD_DIM = 128
X_WIDTH = X_HEADS * X_HEAD_DIM
RMS_EPS = 1e-6
NEG = -1e30
LOG2E = math.log2(math.e)
SLOPE_TERMS = 4

LANES = 128
SUBLANES = 8
HEADS_PER_TILE = LANES // HEAD_DIM
ATT_BLOCK = 256
ROW_TILE = 512
WIDE_ROW_TILE = 1024
COL_CHUNK = 512
VMEM_LIMIT = 56 * 1024 * 1024


def _params(n_axes):
    return pltpu.CompilerParams(
        dimension_semantics=("arbitrary",) * n_axes, vmem_limit_bytes=VMEM_LIMIT)


def _resident(shape):
    return pl.BlockSpec(shape, lambda *_: (0,) * len(shape), pipeline_mode=pl.Buffered(1))


def _rms_scale(x, g):
    y = x * lax.rsqrt(jnp.mean(x * x, axis=-1, keepdims=True) + RMS_EPS)
    return y * g


def _dot(a, b):
    return jnp.dot(a, b, preferred_element_type=F32)


def _dot_nt(a, b):
    return lax.dot_general(a, b, (((1,), (1,)), ((), ())), preferred_element_type=F32)


def _in_proj_kernel(x_ref, g_ref, w_ref, o_ref, *, q_chunks, gate_start):
    n_chunks = w_ref.shape[1] // COL_CHUNK
    half = x_ref.shape[0] // 2
    rows = (slice(0, half), slice(half, 2 * half))
    h, w_bf = {}, {}

    def norm(r):
        h[r] = _rms_scale(x_ref[rows[r], :], g_ref[...]).astype(BF16)

    def chunk(r, c):
        cols = slice(c * COL_CHUNK, (c + 1) * COL_CHUNK)
        if c not in w_bf:
            w_bf[c] = w_ref[:, cols].astype(BF16)
        acc = _dot(h[r], w_bf[c])
        if c in q_chunks:
            acc = acc * (HEAD_DIM ** -0.5 * LOG2E)
        if c >= gate_start:
            acc = 1.0 / (1.0 + jnp.exp(-acc))
        o_ref[rows[r], cols] = acc.astype(o_ref.dtype)

    order = list(range(gate_start, n_chunks)) + list(range(gate_start))
    norm(0)
    chunk(0, order[0])
    norm(1)
    for k in range(1, n_chunks):
        chunk(1, order[k - 1])
        chunk(0, order[k])
    chunk(1, order[-1])


def _in_proj(x2d, g, w):
    n, d = x2d.shape
    cols = w.shape[1]
    q_chunks = (0, 3 * MIX_WIDTH // COL_CHUNK)
    gate_start = 6 * MIX_WIDTH // COL_CHUNK
    return pl.pallas_call(
        functools.partial(_in_proj_kernel, q_chunks=q_chunks, gate_start=gate_start),
        out_shape=jax.ShapeDtypeStruct((n, cols), BF16),
        grid=(n // ROW_TILE,),
        in_specs=[
            pl.BlockSpec((ROW_TILE, d), lambda i: (i, 0)),
            _resident((1, d)),
            _resident((d, cols)),
        ],
        out_specs=pl.BlockSpec((ROW_TILE, cols), lambda i: (i, 0)),
        compiler_params=_params(1),
        name="in_proj",
    )(x2d, g, w)


MOBA_SKEW = 1


def _moba_kernel(q_ref, k_ref, v_ref, kaux_ref, qaux_ref, o_ref,
                 kaug_ref, vaug_t_ref, km_ref):
    seq = q_ref.shape[1]
    nb = seq // ATT_BLOCK
    lane = lax.broadcasted_iota(jnp.int32, (1, LANES), 1)
    key_t = lax.broadcasted_iota(jnp.int32, (ATT_BLOCK, ATT_BLOCK), 0)
    qry_t = lax.broadcasted_iota(jnp.int32, (ATT_BLOCK, ATT_BLOCK), 1)
    causal = key_t <= qry_t
    lane_q = lax.broadcasted_iota(jnp.int32, (ATT_BLOCK, LANES), 1)
    chan_t = lax.broadcasted_iota(jnp.int32, (LANES, ATT_BLOCK), 0)
    nb_rows = km_ref.shape[1]
    blk_row = lax.broadcasted_iota(jnp.int32, (SUBLANES, ATT_BLOCK), 0)

    k_all = k_ref[0]
    v_all = v_ref[0]
    means = [jnp.mean(k_all[j * ATT_BLOCK:(j + 1) * ATT_BLOCK].astype(F32), axis=0, keepdims=True)
             for j in range(nb)]
    means.append(jnp.zeros((nb_rows - nb, LANES), F32))
    means = jnp.concatenate(means, axis=0)
    v_t = v_all.astype(F32).T
    chan_all = lax.broadcasted_iota(jnp.int32, (LANES, seq), 0)
    for hh in range(HEADS_PER_TILE):
        own = (lane >= hh * HEAD_DIM) & (lane < (hh + 1) * HEAD_DIM)
        own_t = (chan_all >= hh * HEAD_DIM) & (chan_all < (hh + 1) * HEAD_DIM)
        kaug_ref[hh] = jnp.where(own, k_all, kaux_ref[hh])
        vaug_t_ref[hh] = jnp.where(own_t, v_t, 1.0).astype(BF16)
        km_ref[hh] = jnp.where(own, means, 0.0).astype(BF16)

    units = [(i, hh) for i in range(nb - 1, -1, -1) for hh in range(HEADS_PER_TILE)]
    n = len(units)
    q_aug, penalty, tiles, col_max, probs, acc, outs = ({} for _ in range(7))

    def prep(u):
        i, hh = units[u]
        q_i = q_ref[0, i * ATT_BLOCK:(i + 1) * ATT_BLOCK, :]
        own_q = (lane_q >= hh * HEAD_DIM) & (lane_q < (hh + 1) * HEAD_DIM)
        q_aug[u] = jnp.where(own_q, q_i, qaux_ref[0, hh:hh + 1, :].astype(BF16))
        if i > MOBA_TOPK:
            gate = _dot_nt(km_ref[hh], q_i)[:SUBLANES]
            beaten = jnp.zeros((SUBLANES, ATT_BLOCK), F32)
            for jp in range(i):
                g_jp = gate[jp:jp + 1, :]
                wins = (g_jp > gate) | ((g_jp == gate) & (jp < blk_row))
                beaten = beaten + jnp.where(wins, 1.0, 0.0)
            chosen = (blk_row < i) & (beaten < MOBA_TOPK)
            penalty[u] = jnp.where(chosen, 0.0, NEG)

    def qk(u):
        i, hh = units[u]
        ts = []
        for j in range(i + 1):
            s = _dot_nt(kaug_ref[hh, j * ATT_BLOCK:(j + 1) * ATT_BLOCK, :], q_aug[u])
            if j == i:
                s = jnp.where(causal, s, NEG)
            elif i > MOBA_TOPK:
                s = s + penalty[u][j:j + 1, :]
            ts.append(s)
        m_el = ts[0]
        for s in ts[1:]:
            m_el = jnp.maximum(m_el, s)
        tiles[u] = ts
        col_max[u] = jnp.max(m_el, axis=0, keepdims=True)

    def expo(u):
        probs[u] = jnp.concatenate([jnp.exp2(s - col_max[u]).astype(BF16) for s in tiles[u]], axis=0)

    def pv(u):
        i, hh = units[u]
        acc[u] = _dot(vaug_t_ref[hh, :, 0:(i + 1) * ATT_BLOCK], probs[u])

    def fin(u):
        i, hh = units[u]
        spare0 = (1 - hh) * HEAD_DIM
        outs[u] = acc[u] / acc[u][spare0:spare0 + 1, :]
        if hh == HEADS_PER_TILE - 1:
            o_pair_t = jnp.where(chan_t < HEAD_DIM, outs[u - 1], outs[u])
            o_ref[0, i * ATT_BLOCK:(i + 1) * ATT_BLOCK, :] = o_pair_t.T.astype(o_ref.dtype)

    stages = (prep, qk, expo, pv, fin)
    for t in range(n + (len(stages) - 1) * MOBA_SKEW):
        for k, stage in enumerate(stages):
            u = t - k * MOBA_SKEW
            if 0 <= u < n:
                stage(u)


def _bf16_terms(value, n_terms):
    terms, rest = [], float(value)
    for _ in range(n_terms):
        term = float(np.asarray(rest, np.float32).astype(jnp.bfloat16).astype(np.float32))
        terms.append(term)
        rest -= term
    return terms


def _moba_aux(seq):
    pos = np.arange(seq)
    lane = np.arange(LANES)
    pairs = MOBA_HEADS // HEADS_PER_TILE
    kaux = np.zeros((HEADS_PER_TILE, seq, LANES), np.float32)
    qaux = np.zeros((pairs, HEADS_PER_TILE, LANES), np.float32)
    for hh in range(HEADS_PER_TILE):
        j = lane - (1 - hh) * HEAD_DIM
        off_lanes = (j >= 0) & (j < SLOPE_TERMS)
        start_lanes = (j >= SLOPE_TERMS) & (j < 2 * SLOPE_TERMS)
        kaux[hh] = (np.where(off_lanes[None, :], (pos % MOBA_BLOCK)[:, None], 0)
                    + np.where(start_lanes[None, :], ((pos // MOBA_BLOCK) * MOBA_BLOCK)[:, None], 0))
        for p in range(pairs):
            slope = 2.0 ** (-8.0 * (p * HEADS_PER_TILE + hh + 1) / MOBA_HEADS)
            terms = _bf16_terms(slope * LOG2E, SLOPE_TERMS)
            for t, term in enumerate(terms):
                qaux[p, hh, (j == t) | (j == t + SLOPE_TERMS)] = term
    return jnp.asarray(kaux, BF16), jnp.asarray(qaux)


def _moba(proj3, q_col, k_col, v_col):
    b, seq, _ = proj3.shape
    nb = seq // MOBA_BLOCK
    assert seq % MOBA_BLOCK == 0 and nb <= SUBLANES
    pairs = MOBA_HEADS // HEADS_PER_TILE
    kaux, qaux = _moba_aux(seq)
    blk = lambda c0: pl.BlockSpec((1, seq, LANES), lambda bi, p: (bi, 0, c0 + p))
    return pl.pallas_call(
        _moba_kernel,
        out_shape=jax.ShapeDtypeStruct((b, seq, MIX_WIDTH), BF16),
        grid=(b, pairs),
        in_specs=[
            blk(q_col // LANES), blk(k_col // LANES), blk(v_col // LANES),
            _resident((HEADS_PER_TILE, seq, LANES)),
            pl.BlockSpec((1, HEADS_PER_TILE, LANES), lambda bi, p: (p, 0, 0)),
        ],
        out_specs=pl.BlockSpec((1, seq, LANES), lambda bi, p: (bi, 0, p)),
        scratch_shapes=[
            pltpu.VMEM((HEADS_PER_TILE, seq, LANES), BF16),
            pltpu.VMEM((HEADS_PER_TILE, LANES, seq), BF16),
            pltpu.VMEM((HEADS_PER_TILE, 2 * SUBLANES * (-(-nb // (2 * SUBLANES))), LANES), BF16),
        ],
        compiler_params=_params(2),
        name="moba",
    )(proj3, proj3, proj3, kaux, qaux)


SB_DEAD = 200.0
SB_SKEW = 2


def _sb_kernel(q_ref, k_ref, v_ref, tri_ref, o_ref, acc_ref, rest_ref):
    seq = q_ref.shape[1]
    nb = seq // ATT_BLOCK
    row_t = lax.broadcasted_iota(jnp.int32, (ATT_BLOCK, ATT_BLOCK), 0)
    col_t = lax.broadcasted_iota(jnp.int32, (ATT_BLOCK, ATT_BLOCK), 1)
    strict = col_t < row_t
    lane_q = lax.broadcasted_iota(jnp.int32, (ATT_BLOCK, LANES), 1)
    tri = tri_ref[...]
    key_rows = lambda j: slice(j * ATT_BLOCK, (j + 1) * ATT_BLOCK)

    def q_own(i, hh):
        q_i = q_ref[0, i * ATT_BLOCK:(i + 1) * ATT_BLOCK, :]
        own_q = (lane_q >= hh * HEAD_DIM) & (lane_q < (hh + 1) * HEAD_DIM)
        return jnp.where(own_q, q_i, jnp.zeros_like(q_i))

    def run_chains(chains, rest_of):
        n = len(chains)
        z, sp_bf, suffix, a = {}, {}, {}, {}
        outs, new_rests = [None] * n, [None] * n

        def scores(c):
            i, hh, j = chains[c]
            z[c] = _dot_nt(q_own(i, hh), k_ref[0, key_rows(j), :])

        def softplus(c):
            i, hh, j = chains[c]
            neg_abs = lax.bitcast_convert_type(
                lax.bitcast_convert_type(z[c], jnp.uint32) | jnp.uint32(0x80000000), F32)
            sp = jnp.maximum(z[c], 0.0) + jnp.log(1.0 + jnp.exp2(neg_abs)) * LOG2E
            if j == i:
                sp = jnp.where(strict, sp, 0.0)
            sp_bf[c] = sp.astype(BF16)

        def suffix_sum(c):
            suffix[c] = _dot(sp_bf[c], tri)

        def weights(c):
            i, hh, j = chains[c]
            rest = rest_of(c, new_rests)
            w = jnp.exp2(z[c] - suffix[c] - rest)
            if j == i:
                w = jnp.where(strict, w, 0.0)
            a[c] = w.astype(BF16)
            new_rests[c] = rest + suffix[c][:, 0:1]

        def values(c):
            i, hh, j = chains[c]
            outs[c] = _dot(a[c], v_ref[0, key_rows(j), :])

        stages = (scores, softplus, suffix_sum, weights, values)
        for t in range(n + (len(stages) - 1) * SB_SKEW):
            for k, stage in enumerate(stages):
                c = t - k * SB_SKEW
                if 0 <= c < n:
                    stage(c)
        return outs, new_rests

    slot = lambda i, hh: i * HEADS_PER_TILE + hh
    units = lambda d: [(i, hh, i - d) for i in range(d, nb) for hh in range(HEADS_PER_TILE)]

    near = units(0) + (units(1) if nb > 1 else [])
    where = {unit: c for c, unit in enumerate(near)}
    zero_rest = jnp.zeros((ATT_BLOCK, 1), F32)

    def near_rest(c, done):
        i, hh, j = near[c]
        return zero_rest if j == i else done[where[(i, hh, i)]]

    outs, rests = run_chains(near, near_rest)
    for (i, hh, j), o, r in zip(near, outs, rests):
        if j == i and (i, hh, i - 1) in where:
            continue
        total = o if j == i else o + outs[where[(i, hh, i)]]
        acc_ref[slot(i, hh)] = total
        rest_ref[slot(i, hh)] = jnp.broadcast_to(r, (ATT_BLOCK, LANES))

    def alive(d):
        low = rest_ref[slot(d, 0)]
        for s in range(slot(d, 0) + 1, nb * HEADS_PER_TILE):
            low = jnp.minimum(low, rest_ref[s])
        return jnp.min(low) < SB_DEAD

    def far_diagonal(d):
        far = units(d)
        outs, rests = run_chains(far, lambda c, _: rest_ref[slot(far[c][0], far[c][1])][:, 0:1])
        for (i, hh, _), o, r in zip(far, outs, rests):
            acc_ref[slot(i, hh)] = acc_ref[slot(i, hh)] + o
            rest_ref[slot(i, hh)] = jnp.broadcast_to(r, (ATT_BLOCK, LANES))

    def from_diagonal(d):
        far_diagonal(d)
        if d + 1 < nb:
            pl.when(alive(d + 1))(functools.partial(from_diagonal, d + 1))

    if nb > 2:
        pl.when(alive(2))(functools.partial(from_diagonal, 2))

    for i in range(nb):
        o_pair = jnp.where(lane_q < HEAD_DIM, acc_ref[slot(i, 0)], acc_ref[slot(i, 1)])
        o_ref[0, i * ATT_BLOCK:(i + 1) * ATT_BLOCK, :] = o_pair.astype(o_ref.dtype)


def _stick_breaking(proj3, q_col, k_col, v_col):
    b, seq, _ = proj3.shape
    assert seq % ATT_BLOCK == 0
    nb = seq // ATT_BLOCK
    pairs = SB_HEADS // HEADS_PER_TILE
    idx = jnp.arange(ATT_BLOCK)
    tri = (idx[:, None] >= idx[None, :]).astype(BF16)
    blk = lambda c0: pl.BlockSpec((1, seq, LANES), lambda bi, p: (bi, 0, c0 + p))
    state = pltpu.VMEM((nb * HEADS_PER_TILE, ATT_BLOCK, LANES), F32)
    return pl.pallas_call(
        _sb_kernel,
        out_shape=jax.ShapeDtypeStruct((b, seq, MIX_WIDTH), BF16),
        grid=(b, pairs),
        in_specs=[
            blk(q_col // LANES), blk(k_col // LANES), blk(v_col // LANES),
            _resident(tri.shape),
        ],
        out_specs=pl.BlockSpec((1, seq, LANES), lambda bi, p: (bi, 0, p)),
        scratch_shapes=[state, state],
        compiler_params=_params(2),
        name="stick_breaking",
    )(proj3, proj3, proj3, tri)


def _mix_cross_kernel(a_ref, b_ref, ga_ref, gb_ref, x_ref, mem_ref, g_ref, gmem_ref,
                      wa_ref, wb_ref, wo_ref, wq_ref, wxo_ref, wkv_ref, o_ref, kv_ref):
    @pl.when(pl.program_id(1) == 0)
    def _():
        mem_n = _rms_scale(mem_ref[0], gmem_ref[...]).astype(BF16)
        kv_ref[...] = _dot(mem_n, wkv_ref[...].astype(BF16)).astype(kv_ref.dtype)

    half = x_ref.shape[1] // 2
    rows = (slice(0, half), slice(half, 2 * half))
    ya, yb, merged, x1, q, probs, denom, attn, cast = ({} for _ in range(9))

    def w16(ref):
        if id(ref) not in cast:
            cast[id(ref)] = ref[...].astype(BF16)
        return cast[id(ref)]
    head_cols = lambda hd: slice(hd * X_HEAD_DIM, (hd + 1) * X_HEAD_DIM)

    def branches(r):
        ya[r] = _dot(a_ref[0, rows[r], :], w16(wa_ref))
        yb[r] = _dot(b_ref[0, rows[r], :], w16(wb_ref))

    def merge(r):
        m = ga_ref[0, rows[r], :].astype(F32) * ya[r] + gb_ref[0, rows[r], :].astype(F32) * yb[r]
        merged[r] = m.astype(BF16)

    def mix_residual(r):
        x1[r] = x_ref[0, rows[r], :] + _dot(merged[r], w16(wo_ref))

    def project(r):
        h = _rms_scale(x1[r], g_ref[...]).astype(BF16)
        q[r] = (_dot(h, w16(wq_ref)) * (X_HEAD_DIM ** -0.5 * LOG2E)).astype(BF16)

    def scores(r):
        probs[r], denom[r] = [], []
        for hd in range(X_HEADS):
            s = _dot_nt(q[r][:, head_cols(hd)], kv_ref[:, head_cols(hd)])
            p = jnp.exp2(s - jnp.max(s, axis=1, keepdims=True))
            denom[r].append(jnp.sum(p, axis=1, keepdims=True))
            probs[r].append(p.astype(BF16))

    def values(r):
        heads = []
        for hd in range(X_HEADS):
            v_h = kv_ref[:, X_WIDTH + hd * X_HEAD_DIM:X_WIDTH + (hd + 1) * X_HEAD_DIM]
            heads.append((_dot(probs[r][hd], v_h) / denom[r][hd]).astype(BF16))
        attn[r] = jnp.concatenate(heads, axis=1)

    def cross_residual(r):
        o_ref[0, rows[r], :] = x1[r] + _dot(attn[r], w16(wxo_ref))

    stages = (branches, merge, mix_residual, project, scores, values, cross_residual)
    for t in range(len(rows) + len(stages) - 1):
        for k, stage in enumerate(stages):
            if 0 <= t - k < len(rows):
                stage(t - k)


def _mix_cross(moba_o, sb_o, proj3, gate_col, x3, mem, g, g_mem, wa, wb, wo, wq, wxo, wkv):
    b, seq, d = x3.shape
    m = mem.shape[1]
    rows = lambda w, c=0: pl.BlockSpec((1, WIDE_ROW_TILE, w), lambda bi, i: (bi, i, c))
    return pl.pallas_call(
        _mix_cross_kernel,
        out_shape=jax.ShapeDtypeStruct((b, seq, d), F32),
        grid=(b, seq // WIDE_ROW_TILE),
        in_specs=[
            rows(MIX_WIDTH), rows(MIX_WIDTH),
            rows(d, gate_col // d), rows(d, gate_col // d + 1),
            rows(d),
            pl.BlockSpec((1, m, d), lambda bi, i: (bi, 0, 0)),
            _resident((1, d)), _resident((1, d)),
            _resident(wa.shape), _resident(wb.shape), _resident(wo.shape),
            _resident(wq.shape), _resident(wxo.shape), _resident(wkv.shape),
        ],
        out_specs=rows(d),
        scratch_shapes=[pltpu.VMEM((m, wkv.shape[1]), BF16)],
        compiler_params=_params(2),
        name="mix_cross",
    )(moba_o, sb_o, proj3, proj3, x3, mem, g, g_mem, wa, wb, wo, wq, wxo, wkv)


def _mlp_kernel(x_ref, g_ref, wu_ref, wd_ref, gf_ref, o_ref, *, final_norm):
    n_chunks = wu_ref.shape[1] // COL_CHUNK
    half = x_ref.shape[0] // 2
    halves = (slice(0, half), slice(half, 2 * half))
    weights = {}

    def mlp_chunk(hx, c):
        cols = slice(c * COL_CHUNK, (c + 1) * COL_CHUNK)
        if c not in weights:
            weights[c] = (wu_ref[:, cols].astype(BF16), wd_ref[cols, :].astype(BF16))
        wu, wd = weights[c]
        u = jnp.maximum(_dot(hx, wu), 0.0)
        return _dot((u * u).astype(BF16), wd)

    hs, accs = [], []
    for rows in halves:
        xr = x_ref[rows, :]
        hr = _rms_scale(xr, g_ref[...]).astype(BF16)
        hs.append(hr)
        accs.append(xr + mlp_chunk(hr, 0))
    h = jnp.concatenate(hs, axis=0)
    acc = jnp.concatenate(accs, axis=0)
    for c in range(1, n_chunks - 1):
        acc = acc + mlp_chunk(h, c)
    for rows in halves:
        out = acc[rows] + mlp_chunk(h[rows], n_chunks - 1)
        o_ref[rows, :] = _rms_scale(out, gf_ref[...]) if final_norm else out


def _mlp(x2d, g, wu, wd, gf, final_norm):
    n, d = x2d.shape
    return pl.pallas_call(
        functools.partial(_mlp_kernel, final_norm=final_norm),
        out_shape=jax.ShapeDtypeStruct((n, d), F32),
        grid=(n // WIDE_ROW_TILE,),
        in_specs=[
            pl.BlockSpec((WIDE_ROW_TILE, d), lambda i: (i, 0)),
            _resident((1, d)), _resident(wu.shape), _resident(wd.shape), _resident((1, d)),
        ],
        out_specs=pl.BlockSpec((WIDE_ROW_TILE, d), lambda i: (i, 0)),
        compiler_params=_params(1),
        name="mlp_final",
    )(x2d, g, wu, wd, gf)


def kernel(x, mem, g_mix, w_in, w_br_moba, w_br_sb, w_out, g_cross, g_mem,
           w_xq, w_xkv, w_xo, g_mlp, w_up, w_down, g_final):
    b, seq, d = x.shape
    n = b * seq
    assert n % ROW_TILE == 0 and seq % WIDE_ROW_TILE == 0
    row = lambda v: v.reshape(1, -1).astype(F32)
    depth = g_mix.shape[0]
    x2d = x.reshape(n, d)
    for l in range(depth):
        proj = _in_proj(x2d, row(g_mix[l]), w_in[l])
        proj3 = proj.reshape(b, seq, -1)
        moba_o = _moba(proj3, 0, MIX_WIDTH, 2 * MIX_WIDTH)
        sb_o = _stick_breaking(proj3, 3 * MIX_WIDTH, 4 * MIX_WIDTH, 5 * MIX_WIDTH)
        x3 = _mix_cross(moba_o, sb_o, proj3, 6 * MIX_WIDTH, x2d.reshape(b, seq, d), mem,
                        row(g_cross[l]), row(g_mem[l]),
                        w_br_moba[l], w_br_sb[l], w_out[l], w_xq[l], w_xo[l], w_xkv[l])
        x2d = _mlp(x3.reshape(n, d), row(g_mlp[l]), w_up[l], w_down[l],
                   row(g_final), final_norm=(l == depth - 1))
    return x2d.reshape(b, seq, d)
```
